```python
import jax, jax.numpy as jnp
from jax import lax
import numpy as np

D_MODEL = 1024
BATCH = 4
SEQ = 8192
DEPTH = 1

GRID_W = 64
EPS = 1e-6
MLSTM_HEADS = 4
MLSTM_INNER = D_MODEL
MLSTM_HEAD_DIM = MLSTM_INNER // MLSTM_HEADS
MLSTM_CHUNK = 64
MLSTM_CONV_W = 5
ATTN_HEAD_DIM = 128
ATTN_HEADS = D_MODEL // ATTN_HEAD_DIM
ATTN_KV_HEADS = 2
ATTN_GROUP = ATTN_HEADS // ATTN_KV_HEADS
ATTN_INNER = ATTN_HEADS * ATTN_HEAD_DIM
ATTN_KV_INNER = ATTN_KV_HEADS * ATTN_HEAD_DIM
Q_BLOCK = 128
ROPE_THETA = 10000.0
ROPE_AXIS_DIM = ATTN_HEAD_DIM // 2
N_BRANCH = 2
D_FF = 4 * D_MODEL
IN_SPLITS = (2 * MLSTM_INNER, MLSTM_INNER, MLSTM_INNER, 4 * MLSTM_HEADS,
             ATTN_INNER, ATTN_KV_INNER, ATTN_KV_INNER, N_BRANCH * D_MODEL)
IN_WIDTH = sum(IN_SPLITS)

kernel_name = 'hybrid_mlstm_gqa2drope_sqrelu_encoder_block'


def rmsnorm(x, w):
    x32 = x.astype(jnp.float32)
    y = x32 * lax.rsqrt(jnp.mean(x32 * x32, axis=-1, keepdims=True) + EPS)
    return (y * w.astype(jnp.float32)).astype(x.dtype)


def split_cols(p, sizes):
    offsets = [int(o) for o in np.cumsum(sizes)[:-1]]
    return jnp.split(p, offsets, axis=-1)


def mlstm_chunked(q, k, v, ig, fg):
    N, S, H, d = q.shape
    L = MLSTM_CHUNK
    NC = S // L
    to_chunks = lambda a: a.reshape(N, NC, L, H, d).transpose(1, 0, 3, 2, 4)
    gate_chunks = lambda a: a.astype(jnp.float32).reshape(N, NC, L, H).transpose(1, 0, 3, 2)
    qc, kc, vc = to_chunks(q), to_chunks(k), to_chunks(v)
    li = gate_chunks(ig)
    b = jnp.cumsum(jax.nn.log_sigmoid(gate_chunks(fg)), axis=-1)
    tril = jnp.tril(jnp.ones((L, L), dtype=bool))

    def step(carry, inp):
        C, n, m = carry
        qb, kb, vb, bb, lib = inp
        log_d = bb[..., :, None] - bb[..., None, :] + lib[..., None, :]
        log_d = jnp.where(tril, log_d, -jnp.inf)
        m_inter = bb + m[..., None]
        m_t = jnp.maximum(jnp.max(log_d, axis=-1), m_inter)
        w = jnp.einsum('nhtd,nhsd->nhts', qb, kb) * jnp.exp(log_d - m_t[..., None])
        inter_w = jnp.exp(m_inter - m_t)
        num = jnp.einsum('nhts,nhsd->nhtd', w, vb) + inter_w[..., None] * jnp.einsum('nhtd,nhde->nhte', qb, C)
        den = jnp.sum(w, axis=-1) + inter_w * jnp.einsum('nhtd,nhd->nht', qb, n)
        h = num / jnp.maximum(jnp.abs(den), jnp.exp(-m_t))[..., None]
        b_last = bb[..., -1]
        log_wk = b_last[..., None] - bb + lib
        m_new = jnp.maximum(b_last + m, jnp.max(log_wk, axis=-1))
        decay = jnp.exp(b_last + m - m_new)
        wk = jnp.exp(log_wk - m_new[..., None])
        C_new = decay[..., None, None] * C + jnp.einsum('nhs,nhsd,nhse->nhde', wk, kb, vb)
        n_new = decay[..., None] * n + jnp.einsum('nhs,nhsd->nhd', wk, kb)
        return (C_new, n_new, m_new), h

    init = (jnp.zeros((N, H, d, d), jnp.float32), jnp.zeros((N, H, d), jnp.float32),
            jnp.full((N, H), -jnp.inf, jnp.float32))
    _, hs = lax.scan(step, init, (qc, kc, vc, b, li))
    return hs.transpose(1, 0, 3, 2, 4).reshape(N, S, H, d)


def mlstm_branch(qk_raw, v, o_pre, gates, conv_w, conv_b, gn_w):
    B, S, _ = v.shape
    H, d = MLSTM_HEADS, MLSTM_HEAD_DIM
    pad = MLSTM_CONV_W // 2
    qk = lax.conv_general_dilated(qk_raw, conv_w[:, None, :], window_strides=(1,), padding=[(pad, pad)],
                                  dimension_numbers=('NWC', 'WIO', 'NWC'),
                                  feature_group_count=2 * MLSTM_INNER) + conv_b
    qk = jax.nn.silu(qk)
    q, k = jnp.split(qk, 2, axis=-1)
    q = q.reshape(B, S, H, d) * (d ** -0.5)
    k = k.reshape(B, S, H, d)
    vh = v.reshape(B, S, H, d)
    i_f, f_f, i_b, f_b = jnp.split(gates, 4, axis=-1)
    flip = lambda a: jnp.flip(a, axis=1)
    h = mlstm_chunked(jnp.concatenate([q, flip(q)], 0), jnp.concatenate([k, flip(k)], 0),
                      jnp.concatenate([vh, flip(vh)], 0), jnp.concatenate([i_f, flip(i_b)], 0),
                      jnp.concatenate([f_f, flip(f_b)], 0))
    h = h[:B] + flip(h[B:])
    mu = jnp.mean(h, axis=-1, keepdims=True)
    var = jnp.mean(jnp.square(h - mu), axis=-1, keepdims=True)
    hn = (h - mu) * lax.rsqrt(var + EPS) * gn_w.astype(jnp.float32).reshape(H, d)
    return hn.reshape(B, S, MLSTM_INNER).astype(v.dtype) * jax.nn.sigmoid(o_pre)


def rope_rotate(x, ang):
    half = x.shape[-1] // 2
    c = jnp.cos(ang)[None, :, None, :].astype(x.dtype)
    s = jnp.sin(ang)[None, :, None, :].astype(x.dtype)
    x1, x2 = x[..., :half], x[..., half:]
    return jnp.concatenate([x1 * c - x2 * s, x1 * s + x2 * c], axis=-1)


def rope_2d(x, row, col):
    n_freq = ROPE_AXIS_DIM // 2
    freqs = ROPE_THETA ** (-jnp.arange(n_freq, dtype=jnp.float32) / n_freq)
    ang_row = row.astype(jnp.float32)[:, None] * freqs[None, :]
    ang_col = col.astype(jnp.float32)[:, None] * freqs[None, :]
    return jnp.concatenate([rope_rotate(x[..., :ROPE_AXIS_DIM], ang_row),
                            rope_rotate(x[..., ROPE_AXIS_DIM:], ang_col)], axis=-1)


def attention_branch(q, k, v, qn_w, kn_w):
    B, S, _ = q.shape
    dh = ATTN_HEAD_DIM
    q = rmsnorm(q.reshape(B, S, ATTN_HEADS, dh), qn_w)
    k = rmsnorm(k.reshape(B, S, ATTN_KV_HEADS, dh), kn_w)
    v = v.reshape(B, S, ATTN_KV_HEADS, dh)
    rows = S // GRID_W
    row = jnp.repeat(jnp.arange(rows, dtype=jnp.int32), GRID_W)
    col = jnp.tile(jnp.arange(GRID_W, dtype=jnp.int32), rows)
    q = rope_2d(q, row, col)
    k = rope_2d(k, row, col)
    nb = S // Q_BLOCK
    qb = q.reshape(B, nb, Q_BLOCK, ATTN_KV_HEADS, ATTN_GROUP, dh).transpose(1, 0, 3, 4, 2, 5)
    kt = k.transpose(0, 2, 1, 3)
    vt = v.transpose(0, 2, 1, 3)
    scale = dh ** -0.5

    def block(qblk):
        s = jnp.einsum('bkgqd,bksd->bkgqs', qblk, kt).astype(jnp.float32) * scale
        p = jax.nn.softmax(s, axis=-1)
        return jnp.einsum('bkgqs,bksd->bkgqd', p.astype(vt.dtype), vt)

    o = lax.map(block, qb)
    return o.transpose(1, 0, 4, 2, 3, 5).reshape(B, S, ATTN_INNER)


def setup_inputs(seed: int = 0) -> dict:
    key = jax.random.key(seed)
    ks = jax.random.split(key, 24)
    nrm = lambda k, shape: jax.random.normal(k, shape, jnp.float32)
    dense = lambda k, fan_in, shape: nrm(k, shape) * (fan_in ** -0.5)
    gain = lambda k, shape: 1.0 + 0.02 * nrm(k, shape)
    i_bias = 0.1 * nrm(ks[8], (DEPTH, 2, 1, MLSTM_HEADS))
    f_bias = jnp.linspace(3.0, 6.0, MLSTM_HEADS, dtype=jnp.float32) + 0.1 * nrm(ks[9], (DEPTH, 2, 1, MLSTM_HEADS))
    b_gates = jnp.concatenate([i_bias, f_bias], axis=2).reshape(DEPTH, 4 * MLSTM_HEADS)
    return {
        'x': nrm(ks[0], (BATCH, SEQ, D_MODEL)),
        'c': nrm(ks[1], (BATCH, D_MODEL)),
        'w_ada': dense(ks[2], D_MODEL, (DEPTH, D_MODEL, 6 * D_MODEL)),
        'b_ada': 0.02 * nrm(ks[3], (DEPTH, 6 * D_MODEL)),
        'norm1_pre': gain(ks[4], (DEPTH, D_MODEL)),
        'norm1_post': gain(ks[5], (DEPTH, D_MODEL)),
        'w_in': dense(ks[6], D_MODEL, (DEPTH, D_MODEL, IN_WIDTH)),
        'b_gates': b_gates,
        'conv_w': dense(ks[10], MLSTM_CONV_W, (DEPTH, MLSTM_CONV_W, 2 * MLSTM_INNER)),
        'conv_b': 0.02 * nrm(ks[11], (DEPTH, 2 * MLSTM_INNER)),
        'mlstm_gn': gain(ks[12], (DEPTH, MLSTM_INNER)),
        'attn_qnorm': gain(ks[13], (DEPTH, ATTN_HEAD_DIM)),
        'attn_knorm': gain(ks[14], (DEPTH, ATTN_HEAD_DIM)),
        'w_branch_m': dense(ks[15], MLSTM_INNER, (DEPTH, MLSTM_INNER, D_MODEL)),
        'w_branch_a': dense(ks[16], ATTN_INNER, (DEPTH, ATTN_INNER, D_MODEL)),
        'w_out': dense(ks[17], D_MODEL, (DEPTH, D_MODEL, D_MODEL)),
        'norm2_pre': gain(ks[18], (DEPTH, D_MODEL)),
        'norm2_post': gain(ks[19], (DEPTH, D_MODEL)),
        'w_mlp_in': dense(ks[20], D_MODEL, (DEPTH, D_MODEL, D_FF)),
        'w_mlp_out': dense(ks[21], D_FF, (DEPTH, D_FF, D_MODEL)),
    }


def reference(x, c, w_ada, b_ada, norm1_pre, norm1_post, w_in, b_gates, conv_w, conv_b, mlstm_gn,
              attn_qnorm, attn_knorm, w_branch_m, w_branch_a, w_out, norm2_pre, norm2_post,
              w_mlp_in, w_mlp_out):
    sc = jax.nn.silu(c)
    for l in range(DEPTH):
        mod = (sc @ w_ada[l] + b_ada[l])[:, None, :]
        shift1, scale1, gate1, shift2, scale2, gate2 = jnp.split(mod, 6, axis=-1)
        h = rmsnorm(x, norm1_pre[l]) * (1.0 + scale1) + shift1
        proj = h @ w_in[l]
        qk_m, v_m, o_m, g_m, q_a, k_a, v_a, br = split_cols(proj, IN_SPLITS)
        y_m = mlstm_branch(qk_m, v_m, o_m, g_m + b_gates[l], conv_w[l], conv_b[l], mlstm_gn[l])
        y_a = attention_branch(q_a, k_a, v_a, attn_qnorm[l], attn_knorm[l])
        g_mlstm, g_attn = jnp.split(jax.nn.sigmoid(br), 2, axis=-1)
        y = g_mlstm * (y_m @ w_branch_m[l]) + g_attn * (y_a @ w_branch_a[l])
        y = y @ w_out[l]
        x = x + gate1 * rmsnorm(y, norm1_post[l])
        h2 = rmsnorm(x, norm2_pre[l]) * (1.0 + scale2) + shift2
        u = jnp.square(jax.nn.relu(h2 @ w_mlp_in[l]))
        x = x + gate2 * rmsnorm(u @ w_mlp_out[l], norm2_post[l])
    return x
```

```python
import functools
import math

import jax
import jax.numpy as jnp
from jax import lax
from jax.experimental import pallas as pl
from jax.experimental.pallas import tpu as pltpu

EPS = 1e-6
GRID_W = 64
ROPE_THETA = 10000.0
MLSTM_HEADS = 4
MLSTM_CONV_W = 5
ATTN_HEAD_DIM = 128
ATTN_KV_HEADS = 2
LANES = 128
HALO = 16

F32 = jnp.float32
BF16 = jnp.bfloat16
VMEM_LIMIT = 56 * 1024 * 1024


def _params(*sem):
    return pltpu.CompilerParams(dimension_semantics=sem, vmem_limit_bytes=VMEM_LIMIT)


def _sigmoid(x):
    return 1.0 / (1.0 + jnp.exp(-x))


def _dot(a, b):
    return jnp.dot(a, b, preferred_element_type=F32)


def _dot_nt(a, b):
    return lax.dot_general(a, b, (((1,), (1,)), ((), ())), preferred_element_type=F32)


def _dot_tn(a, b):
    return lax.dot_general(a, b, (((0,), (0,)), ((), ())), preferred_element_type=F32)


def _ada_kernel(c_ref, w_ref, b_ref, o_ref):
    c = c_ref[...]
    sc = c * _sigmoid(c)
    o_ref[...] = jnp.dot(sc, w_ref[...], preferred_element_type=F32,
                         precision=lax.Precision.HIGHEST) + b_ref[...]


def _ada(c, w, b):
    bsz, d = c.shape
    n = w.shape[1]
    tn = 1024
    return pl.pallas_call(
        _ada_kernel,
        grid=(n // tn,),
        in_specs=[pl.BlockSpec((bsz, d), lambda j: (0, 0)),
                  pl.BlockSpec((d, tn), lambda j: (0, j)),
                  pl.BlockSpec((1, tn), lambda j: (0, j))],
        out_specs=pl.BlockSpec((bsz, tn), lambda j: (0, j)),
        out_shape=jax.ShapeDtypeStruct((bsz, n), F32),
        compiler_params=_params("parallel"),
        name="ada",
    )(c, w, b.reshape(1, n))


def _in_proj_kernel(x_ref, mod_ref, nw_ref, w_ref, wg_ref, bg_ref, o_ref, g_ref, h_scr):
    d = x_ref.shape[-1]

    @pl.when(pl.program_id(2) == 0)
    def _():
        x = x_ref[...]
        ms = jnp.mean(x * x, axis=-1, keepdims=True)
        shift = mod_ref[:, 0:d]
        scale = mod_ref[:, d:2 * d]
        h = x * lax.rsqrt(ms + EPS) * nw_ref[...] * (1.0 + scale) + shift
        hb = h.astype(BF16)
        h_scr[...] = hb
        g_ref[...] = _dot(hb, wg_ref[...]) + bg_ref[...]

    o_ref[...] = _dot(h_scr[...], w_ref[...]).astype(o_ref.dtype)


def _in_proj(x, mod3, norm_w, w_main, w_gates, b_gates, tm, tn):
    bsz, s, d = x.shape
    n = w_main.shape[1]
    ng = w_gates.shape[1]
    return pl.pallas_call(
        _in_proj_kernel,
        grid=(bsz, s // tm, n // tn),
        in_specs=[pl.BlockSpec((None, tm, d), lambda b, m, j: (b, m, 0)),
                  pl.BlockSpec((None, 1, mod3.shape[-1]), lambda b, m, j: (b, 0, 0)),
                  pl.BlockSpec((1, d), lambda b, m, j: (0, 0)),
                  pl.BlockSpec((d, tn), lambda b, m, j: (0, j)),
                  pl.BlockSpec((d, ng), lambda b, m, j: (0, 0)),
                  pl.BlockSpec((1, ng), lambda b, m, j: (0, 0))],
        out_specs=[pl.BlockSpec((None, tm, tn), lambda b, m, j: (b, m, j)),
                   pl.BlockSpec((None, tm, ng), lambda b, m, j: (b, m, 0))],
        out_shape=[jax.ShapeDtypeStruct((bsz, s, n), BF16),
                   jax.ShapeDtypeStruct((bsz, s, ng), F32)],
        scratch_shapes=[pltpu.VMEM((tm, d), BF16)],
        compiler_params=_params("parallel", "parallel", "arbitrary"),
        name="in_proj",
    )(x, mod3, norm_w, w_main, w_gates, b_gates)


def _conv_kernel(prev_ref, cur_ref, next_ref, w_ref, b_ref, o_ref, *, q_blocks, q_scale):
    t = pl.program_id(1)
    nt = pl.num_programs(1)
    ts = cur_ref.shape[0]
    half = HALO // 2
    prev = prev_ref[...].astype(F32)[half:HALO, :] * (t > 0).astype(F32)
    nxt = next_ref[...].astype(F32)[0:half, :] * (t < nt - 1).astype(F32)
    e = jnp.concatenate([prev, cur_ref[...].astype(F32), nxt], axis=0)
    rows = ts + HALO
    pad = MLSTM_CONV_W // 2
    acc = None
    for j in range(MLSTM_CONV_W):
        shifted = e if j == pad else pltpu.roll(e, (pad - j) % rows, 0)
        term = shifted[half:half + ts, :] * w_ref[j:j + 1, :]
        acc = term if acc is None else acc + term
    y = acc + b_ref[...]
    y = y * _sigmoid(y)
    scale = jnp.where(pl.program_id(2) < q_blocks, q_scale, 1.0).astype(F32)
    o_ref[...] = (y * scale).astype(o_ref.dtype)


def _conv(proj, conv_w, conv_b, ts, cw, q_scale):
    bsz, s, _ = proj.shape
    c = conv_w.shape[1]
    kern = functools.partial(_conv_kernel, q_blocks=(c // 2) // cw, q_scale=q_scale)
    nh = s // HALO
    return pl.pallas_call(
        kern,
        grid=(bsz, s // ts, c // cw),
        in_specs=[pl.BlockSpec((None, HALO, cw),
                               lambda b, t, j: (b, jnp.maximum(t * (ts // HALO) - 1, 0), j)),
                  pl.BlockSpec((None, ts, cw), lambda b, t, j: (b, t, j)),
                  pl.BlockSpec((None, HALO, cw),
                               lambda b, t, j: (b, jnp.minimum((t + 1) * (ts // HALO), nh - 1), j)),
                  pl.BlockSpec((MLSTM_CONV_W, cw), lambda b, t, j: (0, j)),
                  pl.BlockSpec((1, cw), lambda b, t, j: (0, j))],
        out_specs=pl.BlockSpec((None, ts, cw), lambda b, t, j: (b, t, j)),
        out_shape=jax.ShapeDtypeStruct((bsz, s, c), BF16),
        compiler_params=_params("parallel", "parallel", "parallel"),
        name="conv",
    )(proj, proj, proj, conv_w, conv_b.reshape(1, c))


def _mlstm_kernel(qkf_ref, qkb_ref, vf_ref, vb_ref, gf_ref, gb_ref, hf_ref, hb_ref,
                  c_scr, n_scr, m_scr):
    ln = vf_ref.shape[0]
    inner = vf_ref.shape[1]
    dh = inner // MLSTM_HEADS

    @pl.when(pl.program_id(1) == 0)
    def _():
        c_scr[...] = jnp.zeros_like(c_scr)
        n_scr[...] = jnp.zeros_like(n_scr)
        m_scr[...] = jnp.full_like(m_scr, -jnp.inf)

    row = lax.broadcasted_iota(jnp.int32, (ln, ln), 0)
    col = lax.broadcasted_iota(jnp.int32, (ln, ln), 1)
    lower = col <= row
    upper = col >= row

    def gate_terms(g_ref, tri):
        gi = g_ref[:, 0:LANES]
        gf = g_ref[:, LANES:2 * LANES]
        a = jnp.minimum(gf, 0.0) - jnp.log(1.0 + jnp.exp(-jnp.abs(gf)))
        b = jnp.dot(tri.astype(F32), a, preferred_element_type=F32, precision=lax.Precision.HIGHEST)
        u = gi - b
        return b, u, u.T

    b_f, u_f, ut_f = gate_terms(gf_ref, lower)
    b_b, u_b, ut_b = gate_terms(gb_ref, upper)

    for direction in range(2):
        qk_ref, v_ref, h_ref = ((qkf_ref, vf_ref, hf_ref), (qkb_ref, vb_ref, hb_ref))[direction]
        bmat, umat, utmat = ((b_f, u_f, ut_f), (b_b, u_b, ut_b))[direction]
        mask = (lower, upper)[direction]
        for head in range(MLSTM_HEADS):
            j = direction * MLSTM_HEADS + head
            q = qk_ref[:, head * dh:(head + 1) * dh]
            k = qk_ref[:, inner + head * dh:inner + (head + 1) * dh]
            v = v_ref[:, head * dh:(head + 1) * dh]
            b_col = bmat[:, j:j + 1]
            u_col = umat[:, j:j + 1]
            u_row = utmat[j:j + 1, :]
            m_prev = m_scr[j:j + 1, 0:1]
            c_old = c_scr[j]
            n_old = n_scr[j:j + 1, :]

            u_mat = jnp.where(mask, u_row, -jnp.inf)
            g_col = jnp.maximum(jnp.max(u_mat, axis=1, keepdims=True), m_prev)
            decay_mat = jnp.exp(u_mat - g_col)
            w = _dot_nt(q, k) * decay_mat
            inter = jnp.exp(m_prev - g_col)
            num = _dot(w.astype(BF16), v) + inter * _dot(q, c_old.astype(BF16))
            den = (jnp.sum(w, axis=1, keepdims=True)
                   + inter * jnp.sum(q.astype(F32) * n_old, axis=1, keepdims=True))
            floor = jnp.exp(-(g_col + b_col))
            h_ref[:, head * dh:(head + 1) * dh] = num / jnp.maximum(jnp.abs(den), floor)

            g_end = jnp.max(g_col, axis=0, keepdims=True)
            b_tot = jnp.min(b_col, axis=0, keepdims=True)
            carry = jnp.exp(m_prev - g_end)
            kw = k.astype(F32) * jnp.exp(u_col - g_end)
            c_scr[j] = carry * c_old + _dot_tn(kw.astype(BF16), v)
            n_scr[j:j + 1, :] = carry * n_old + jnp.sum(kw, axis=0, keepdims=True)
            m_scr[j:j + 1, :] = jnp.broadcast_to(b_tot + g_end, (1, m_scr.shape[1]))


def _mlstm(qk, proj, gates, v_block, ln):
    bsz, s, two_inner = qk.shape
    inner = two_inner // 2
    nc = s // ln
    ng = gates.shape[-1]
    dh = inner // MLSTM_HEADS
    nchain = 2 * MLSTM_HEADS
    fwd = lambda b, i: (b, i, 0)
    bwd = lambda b, i: (b, nc - 1 - i, 0)
    return pl.pallas_call(
        _mlstm_kernel,
        grid=(bsz, nc),
        in_specs=[pl.BlockSpec((None, ln, two_inner), fwd),
                  pl.BlockSpec((None, ln, two_inner), bwd),
                  pl.BlockSpec((None, ln, inner), lambda b, i: (b, i, v_block)),
                  pl.BlockSpec((None, ln, inner), lambda b, i: (b, nc - 1 - i, v_block)),
                  pl.BlockSpec((None, ln, ng), fwd),
                  pl.BlockSpec((None, ln, ng), bwd)],
        out_specs=[pl.BlockSpec((None, ln, inner), fwd),
                   pl.BlockSpec((None, ln, inner), bwd)],
        out_shape=[jax.ShapeDtypeStruct((bsz, s, inner), F32),
                   jax.ShapeDtypeStruct((bsz, s, inner), F32)],
        scratch_shapes=[pltpu.VMEM((nchain, dh, dh), F32),
                        pltpu.VMEM((nchain, dh), F32),
                        pltpu.VMEM((nchain, LANES), F32)],
        compiler_params=_params("parallel", "arbitrary"),
        name="mlstm",
    )(qk, qk, proj, proj, gates, gates)


def _rope_tab_kernel(freq_ref, cos_ref, sin_ref):
    ts = cos_ref.shape[0]
    t = pl.program_id(0) * ts + lax.broadcasted_iota(jnp.int32, (ts, LANES), 0)
    lane = lax.broadcasted_iota(jnp.int32, (ts, LANES), 1)
    grid_shift = GRID_W.bit_length() - 1
    pos = jnp.where(lane >= LANES // 2, t & (GRID_W - 1), t >> grid_shift).astype(F32)
    ang = pos * freq_ref[...]
    sign = jnp.where((lane & (LANES // 4)) == 0, -1.0, 1.0)
    cos_ref[...] = jnp.cos(ang)
    sin_ref[...] = jnp.sin(ang) * sign


def _rope_tables(s):
    n_freq = ATTN_HEAD_DIM // 4
    freqs = ROPE_THETA ** (-jnp.arange(n_freq, dtype=F32) / n_freq)
    freq_lanes = jnp.tile(freqs, 4).reshape(1, ATTN_HEAD_DIM)
    ts = min(s, 1024)
    return pl.pallas_call(
        _rope_tab_kernel,
        grid=(s // ts,),
        in_specs=[pl.BlockSpec((1, LANES), lambda t: (0, 0))],
        out_specs=[pl.BlockSpec((ts, LANES), lambda t: (t, 0)),
                   pl.BlockSpec((ts, LANES), lambda t: (t, 0))],
        out_shape=[jax.ShapeDtypeStruct((s, LANES), F32),
                   jax.ShapeDtypeStruct((s, LANES), F32)],
        compiler_params=_params("parallel"),
        name="rope_tab",
    )(freq_lanes)


def _norm_rope(x, w, cos, sin, lane):
    ms = jnp.mean(x * x, axis=-1, keepdims=True)
    y = x * lax.rsqrt(ms + EPS) * w
    quarter = LANES // 4
    swapped = jnp.where((lane & quarter) == 0,
                        pltpu.roll(y, LANES - quarter, 1), pltpu.roll(y, quarter, 1))
    return y * cos + swapped * sin


def _qk_prep_kernel(q_ref, k_ref, cos_ref, sin_ref, qw_ref, kw_ref, qo_ref, ko_ref, *, q_scale):
    ts = q_ref.shape[0]
    cos = cos_ref[...]
    sin = sin_ref[...]
    lane = lax.broadcasted_iota(jnp.int32, (ts, LANES), 1)
    for h in range(q_ref.shape[1] // LANES):
        sl = slice(h * LANES, (h + 1) * LANES)
        y = _norm_rope(q_ref[:, sl].astype(F32), qw_ref[...], cos, sin, lane)
        qo_ref[:, sl] = (y * q_scale).astype(qo_ref.dtype)
    for h in range(k_ref.shape[1] // LANES):
        sl = slice(h * LANES, (h + 1) * LANES)
        y = _norm_rope(k_ref[:, sl].astype(F32), kw_ref[...], cos, sin, lane)
        ko_ref[:, sl] = y.astype(ko_ref.dtype)


def _qk_prep(proj, cos, sin, qn_w, kn_w, q_block, k_block, q_inner, kv_inner, ts, q_scale):
    bsz, s, _ = proj.shape
    kern = functools.partial(_qk_prep_kernel, q_scale=q_scale)
    return pl.pallas_call(
        kern,
        grid=(bsz, s // ts),
        in_specs=[pl.BlockSpec((None, ts, q_inner), lambda b, t: (b, t, q_block)),
                  pl.BlockSpec((None, ts, kv_inner), lambda b, t: (b, t, k_block)),
                  pl.BlockSpec((ts, LANES), lambda b, t: (t, 0)),
                  pl.BlockSpec((ts, LANES), lambda b, t: (t, 0)),
                  pl.BlockSpec((1, LANES), lambda b, t: (0, 0)),
                  pl.BlockSpec((1, LANES), lambda b, t: (0, 0))],
        out_specs=[pl.BlockSpec((None, ts, q_inner), lambda b, t: (b, t, 0)),
                   pl.BlockSpec((None, ts, kv_inner), lambda b, t: (b, t, 0))],
        out_shape=[jax.ShapeDtypeStruct((bsz, s, q_inner), BF16),
                   jax.ShapeDtypeStruct((bsz, s, kv_inner), BF16)],
        compiler_params=_params("parallel", "parallel"),
        name="qk_prep",
    )(proj, proj, cos, sin, qn_w.reshape(1, LANES), kn_w.reshape(1, LANES))


def _attn_kernel(q_ref, k_ref, v_ref, o_ref, m_scr, l_scr, acc_scr):
    ki = pl.program_id(3)
    group = q_ref.shape[1] // LANES

    @pl.when(ki == 0)
    def _():
        m_scr[...] = jnp.full_like(m_scr, -jnp.inf)
        l_scr[...] = jnp.zeros_like(l_scr)
        acc_scr[...] = jnp.zeros_like(acc_scr)

    k = k_ref[...]
    v = v_ref[...]
    for h in range(group):
        q = q_ref[:, h * LANES:(h + 1) * LANES]
        s = _dot_nt(q, k)
        m_old = m_scr[h]
        m_new = jnp.maximum(m_old, jnp.max(s, axis=-1, keepdims=True))
        alpha = jnp.exp(m_old - m_new)
        p = jnp.exp(s - m_new)
        l_scr[h] = alpha * l_scr[h] + jnp.sum(p, axis=-1, keepdims=True)
        acc_scr[h] = alpha * acc_scr[h] + _dot(p.astype(BF16), v)
        m_scr[h] = m_new

    @pl.when(ki == pl.num_programs(3) - 1)
    def _():
        for h in range(group):
            o_ref[:, h * LANES:(h + 1) * LANES] = (acc_scr[h] / l_scr[h]).astype(o_ref.dtype)


def _attention(q, k, proj, v_block0, tq, tk):
    bsz, s, q_inner = q.shape
    kvh = k.shape[-1] // ATTN_HEAD_DIM
    gw = q_inner // kvh
    group = gw // ATTN_HEAD_DIM
    return pl.pallas_call(
        _attn_kernel,
        grid=(bsz, kvh, s // tq, s // tk),
        in_specs=[pl.BlockSpec((None, tq, gw), lambda b, g, i, j: (b, i, g)),
                  pl.BlockSpec((None, tk, ATTN_HEAD_DIM), lambda b, g, i, j: (b, j, g)),
                  pl.BlockSpec((None, tk, ATTN_HEAD_DIM), lambda b, g, i, j: (b, j, v_block0 + g))],
        out_specs=pl.BlockSpec((None, tq, gw), lambda b, g, i, j: (b, i, g)),
        out_shape=jax.ShapeDtypeStruct((bsz, s, q_inner), BF16),
        scratch_shapes=[pltpu.VMEM((group, tq, 1), F32),
                        pltpu.VMEM((group, tq, 1), F32),
                        pltpu.VMEM((group, tq, ATTN_HEAD_DIM), F32)],
        compiler_params=_params("parallel", "parallel", "parallel", "arbitrary"),
        name="attn",
    )(q, k, proj)


def _merge_kernel(hf_ref, hb_ref, o_ref, bm_ref, ba_ref, ya_ref, x_ref, mod_ref, gn_ref, nw_ref,
                  wm_ref, wa_ref, wo_ref, out_ref, ym_scr):
    d = x_ref.shape[-1]
    dh = d // MLSTM_HEADS
    for head in range(MLSTM_HEADS):
        sl = slice(head * dh, (head + 1) * dh)
        h = hf_ref[:, sl] + hb_ref[:, sl]
        mu = jnp.mean(h, axis=-1, keepdims=True)
        hc = h - mu
        var = jnp.mean(hc * hc, axis=-1, keepdims=True)
        hn = hc * lax.rsqrt(var + EPS) * gn_ref[:, sl]
        ym_scr[:, sl] = (hn * _sigmoid(o_ref[:, sl].astype(F32))).astype(BF16)
    pm = _dot(ym_scr[...], wm_ref[...])
    pa = _dot(ya_ref[...], wa_ref[...])
    y = _sigmoid(bm_ref[...].astype(F32)) * pm + _sigmoid(ba_ref[...].astype(F32)) * pa
    y2 = _dot(y.astype(BF16), wo_ref[...])
    ms = jnp.mean(y2 * y2, axis=-1, keepdims=True)
    gate = mod_ref[:, 2 * d:3 * d]
    out_ref[...] = x_ref[...] + gate * (y2 * lax.rsqrt(ms + EPS) * nw_ref[...])


def _merge(hf, hb, proj, ya, x, mod3, gn_w, norm_w, wm, wa, wo, o_block, bm_block, ba_block, tm):
    bsz, s, d = x.shape
    tok = lambda b, t: (b, t, 0)
    const = lambda b, t: (0, 0)
    return pl.pallas_call(
        _merge_kernel,
        grid=(bsz, s // tm),
        in_specs=[pl.BlockSpec((None, tm, d), tok),
                  pl.BlockSpec((None, tm, d), tok),
                  pl.BlockSpec((None, tm, d), lambda b, t: (b, t, o_block)),
                  pl.BlockSpec((None, tm, d), lambda b, t: (b, t, bm_block)),
                  pl.BlockSpec((None, tm, d), lambda b, t: (b, t, ba_block)),
                  pl.BlockSpec((None, tm, d), tok),
                  pl.BlockSpec((None, tm, d), tok),
                  pl.BlockSpec((None, 1, mod3.shape[-1]), lambda b, t: (b, 0, 0)),
                  pl.BlockSpec((1, d), const),
                  pl.BlockSpec((1, d), const),
                  pl.BlockSpec((d, d), const),
                  pl.BlockSpec((d, d), const),
                  pl.BlockSpec((d, d), const)],
        out_specs=pl.BlockSpec((None, tm, d), tok),
        out_shape=jax.ShapeDtypeStruct((bsz, s, d), F32),
        scratch_shapes=[pltpu.VMEM((tm, d), BF16)],
        compiler_params=_params("parallel", "parallel"),
        name="merge",
    )(hf, hb, proj, proj, proj, ya, x, mod3, gn_w, norm_w, wm, wa, wo)


def _mlp_kernel(x_ref, mod_ref, n1_ref, n2_ref, w1_ref, w2_ref, out_ref, *, ff_chunk):
    d = x_ref.shape[-1]
    x = x_ref[...]
    ms = jnp.mean(x * x, axis=-1, keepdims=True)
    shift = mod_ref[:, 3 * d:4 * d]
    scale = mod_ref[:, 4 * d:5 * d]
    gate = mod_ref[:, 5 * d:6 * d]
    h = (x * lax.rsqrt(ms + EPS) * n1_ref[...] * (1.0 + scale) + shift).astype(BF16)
    acc = None
    for c in range(w1_ref.shape[1] // ff_chunk):
        sl = slice(c * ff_chunk, (c + 1) * ff_chunk)
        u = jnp.maximum(_dot(h, w1_ref[:, sl]), 0.0)
        part = _dot((u * u).astype(BF16), w2_ref[sl, :])
        acc = part if acc is None else acc + part
    ms2 = jnp.mean(acc * acc, axis=-1, keepdims=True)
    out_ref[...] = x + gate * (acc * lax.rsqrt(ms2 + EPS) * n2_ref[...])


def _mlp(x, mod3, n1, n2, w1, w2, tm, ff_chunk):
    bsz, s, d = x.shape
    ff = w1.shape[1]
    tok = lambda b, t: (b, t, 0)
    const = lambda b, t: (0, 0)
    kern = functools.partial(_mlp_kernel, ff_chunk=ff_chunk)
    return pl.pallas_call(
        kern,
        grid=(bsz, s // tm),
        in_specs=[pl.BlockSpec((None, tm, d), tok),
                  pl.BlockSpec((None, 1, mod3.shape[-1]), lambda b, t: (b, 0, 0)),
                  pl.BlockSpec((1, d), const),
                  pl.BlockSpec((1, d), const),
                  pl.BlockSpec((d, ff), const, pipeline_mode=pl.Buffered(1)),
                  pl.BlockSpec((ff, d), const, pipeline_mode=pl.Buffered(1))],
        out_specs=pl.BlockSpec((None, tm, d), tok),
        out_shape=jax.ShapeDtypeStruct((bsz, s, d), F32),
        compiler_params=_params("parallel", "parallel"),
        name="mlp",
    )(x, mod3, n1, n2, w1, w2)


def _layer(x, mod, norm1_pre, norm1_post, w_in, b_gates, conv_w, conv_b, mlstm_gn, attn_qnorm,
           attn_knorm, w_branch_m, w_branch_a, w_out, norm2_pre, norm2_post, w_mlp_in, w_mlp_out):
    bsz, s, d = x.shape
    inner = d
    n_gate = 4 * MLSTM_HEADS
    q_inner = d
    kv_inner = ATTN_KV_HEADS * ATTN_HEAD_DIM

    o_qk, o_v, o_o = 0, 2 * inner, 3 * inner
    o_g = 4 * inner
    o_qa = o_g + n_gate
    o_ka = o_qa + q_inner
    o_va = o_ka + kv_inner
    o_br = o_va + kv_inner
    w_main = jnp.concatenate([w_in[:, o_qk:o_g], w_in[:, o_qa:o_ka], w_in[:, o_br:o_br + 2 * d],
                              w_in[:, o_ka:o_va], w_in[:, o_va:o_br]], axis=1).astype(BF16)
    c_v, c_o, c_qa, c_bm, c_ba = 2 * inner, 3 * inner, 4 * inner, 5 * inner, 6 * inner
    c_ka = 7 * inner
    c_va = c_ka + kv_inner
    hm = MLSTM_HEADS
    wg = w_in[:, o_g:o_qa]
    gate_cols = lambda a: (jnp.concatenate([a[..., 0:hm], a[..., 2 * hm:3 * hm]], -1),
                           jnp.concatenate([a[..., hm:2 * hm], a[..., 3 * hm:4 * hm]], -1))
    wi, wf = gate_cols(wg)
    bi, bf = gate_cols(b_gates)
    lane_pad = lambda a: jnp.pad(a, [(0, 0)] * (a.ndim - 1) + [(0, LANES - a.shape[-1])])
    w_gates = jnp.concatenate([lane_pad(wi), lane_pad(wf)], -1).astype(BF16)
    bias_gates = jnp.concatenate([lane_pad(bi), lane_pad(bf)], -1).reshape(1, 2 * LANES)

    mod3 = mod.reshape(bsz, 1, mod.shape[-1])
    tm = min(s, 1024)
    proj, gates = _in_proj(x, mod3, norm1_pre.reshape(1, d), w_main, w_gates, bias_gates,
                           tm=tm, tn=1536)

    dh_m = inner // MLSTM_HEADS
    qk = _conv(proj, conv_w, conv_b, ts=min(s, 512), cw=512, q_scale=dh_m ** -0.5)
    h_f, h_b = _mlstm(qk, proj, gates, v_block=c_v // inner, ln=min(s, 256))

    cos, sin = _rope_tables(s)
    q_rot, k_rot = _qk_prep(proj, cos, sin, attn_qnorm, attn_knorm, q_block=c_qa // q_inner,
                            k_block=c_ka // kv_inner, q_inner=q_inner, kv_inner=kv_inner,
                            ts=min(s, 512), q_scale=ATTN_HEAD_DIM ** -0.5)
    y_a = _attention(q_rot, k_rot, proj, v_block0=c_va // ATTN_HEAD_DIM,
                     tq=min(s, 512), tk=min(s, 1024))

    x1 = _merge(h_f, h_b, proj, y_a, x, mod3, mlstm_gn.reshape(1, d), norm1_post.reshape(1, d),
                w_branch_m.astype(BF16), w_branch_a.astype(BF16), w_out.astype(BF16),
                o_block=c_o // d, bm_block=c_bm // d, ba_block=c_ba // d, tm=min(s, 512))
    return _mlp(x1, mod3, norm2_pre.reshape(1, d), norm2_post.reshape(1, d),
                w_mlp_in.astype(BF16), w_mlp_out.astype(BF16), tm=min(s, 512), ff_chunk=1024)


def kernel(x, c, w_ada, b_ada, norm1_pre, norm1_post, w_in, b_gates, conv_w, conv_b, mlstm_gn, attn_qnorm, attn_knorm, w_branch_m, w_branch_a, w_out, norm2_pre, norm2_post, w_mlp_in, w_mlp_out):
    for l in range(w_ada.shape[0]):
        mod = _ada(c, w_ada[l], b_ada[l])
        x = _layer(x, mod, norm1_pre[l], norm1_post[l], w_in[l], b_gates[l], conv_w[l], conv_b[l],
                   mlstm_gn[l], attn_qnorm[l], attn_knorm[l], w_branch_m[l], w_branch_a[l], w_out[l],
                   norm2_pre[l], norm2_post[l], w_mlp_in[l], w_mlp_out[l])
    return x
```

```python
import functools
import math

import jax
import jax.numpy as jnp
from jax import lax
from jax.experimental import pallas as pl
from jax.experimental.pallas import tpu as pltpu

EPS = 1e-6
GRID_W = 64
ROPE_THETA = 10000.0
MLSTM_HEADS = 4
MLSTM_CONV_W = 5
ATTN_HEAD_DIM = 128
ATTN_KV_HEADS = 2
LANES = 128
HALO = 16

F32 = jnp.float32
BF16 = jnp.bfloat16
VMEM_LIMIT = 56 * 1024 * 1024


def _params(*sem):
    return pltpu.CompilerParams(dimension_semantics=sem, vmem_limit_bytes=VMEM_LIMIT)


def _sigmoid(x):
    return 1.0 / (1.0 + jnp.exp(-x))


def _dot(a, b):
    return jnp.dot(a, b, preferred_element_type=F32)


def _dot_nt(a, b):
    return lax.dot_general(a, b, (((1,), (1,)), ((), ())), preferred_element_type=F32)


def _dot_tn(a, b):
    return lax.dot_general(a, b, (((0,), (0,)), ((), ())), preferred_element_type=F32)


def _ada_kernel(c_ref, w_ref, b_ref, o_ref):
    c = c_ref[...]
    sc = c * _sigmoid(c)
    o_ref[...] = jnp.dot(sc, w_ref[...], preferred_element_type=F32,
                         precision=lax.Precision.HIGHEST) + b_ref[...]


def _ada(c, w, b):
    bsz, d = c.shape
    n = w.shape[1]
    tn = 1024
    return pl.pallas_call(
        _ada_kernel,
        grid=(n // tn,),
        in_specs=[pl.BlockSpec((bsz, d), lambda j: (0, 0)),
                  pl.BlockSpec((d, tn), lambda j: (0, j)),
                  pl.BlockSpec((1, tn), lambda j: (0, j))],
        out_specs=pl.BlockSpec((bsz, tn), lambda j: (0, j)),
        out_shape=jax.ShapeDtypeStruct((bsz, n), F32),
        compiler_params=_params("parallel"),
        name="ada",
    )(c, w, b.reshape(1, n))


def _in_proj_kernel(x_ref, mod_ref, nw_ref, w_ref, wg_ref, bg_ref, o_ref, g_ref, h_scr):
    d = x_ref.shape[-1]

    @pl.when(pl.program_id(2) == 0)
    def _():
        x = x_ref[...]
        ms = jnp.mean(x * x, axis=-1, keepdims=True)
        shift = mod_ref[:, 0:d]
        scale = mod_ref[:, d:2 * d]
        h = x * lax.rsqrt(ms + EPS) * nw_ref[...] * (1.0 + scale) + shift
        hb = h.astype(BF16)
        h_scr[...] = hb
        g_ref[...] = _dot(hb, wg_ref[...]) + bg_ref[...]

    o_ref[...] = _dot(h_scr[...], w_ref[...]).astype(o_ref.dtype)


def _in_proj(x, mod3, norm_w, w_main, w_gates, b_gates, tm, tn):
    bsz, s, d = x.shape
    n = w_main.shape[1]
    ng = w_gates.shape[1]
    return pl.pallas_call(
        _in_proj_kernel,
        grid=(bsz, s // tm, n // tn),
        in_specs=[pl.BlockSpec((None, tm, d), lambda b, m, j: (b, m, 0)),
                  pl.BlockSpec((None, 1, mod3.shape[-1]), lambda b, m, j: (b, 0, 0)),
                  pl.BlockSpec((1, d), lambda b, m, j: (0, 0)),
                  pl.BlockSpec((d, tn), lambda b, m, j: (0, j)),
                  pl.BlockSpec((d, ng), lambda b, m, j: (0, 0)),
                  pl.BlockSpec((1, ng), lambda b, m, j: (0, 0))],
        out_specs=[pl.BlockSpec((None, tm, tn), lambda b, m, j: (b, m, j)),
                   pl.BlockSpec((None, tm, ng), lambda b, m, j: (b, m, 0))],
        out_shape=[jax.ShapeDtypeStruct((bsz, s, n), BF16),
                   jax.ShapeDtypeStruct((bsz, s, ng), F32)],
        scratch_shapes=[pltpu.VMEM((tm, d), BF16)],
        compiler_params=_params("parallel", "parallel", "arbitrary"),
        name="in_proj",
    )(x, mod3, norm_w, w_main, w_gates, b_gates)


def _conv_kernel(prev_ref, cur_ref, next_ref, w_ref, b_ref, o_ref, *, q_blocks, q_scale):
    t = pl.program_id(1)
    nt = pl.num_programs(1)
    ts = cur_ref.shape[0]
    half = HALO // 2
    prev = prev_ref[...].astype(F32)[half:HALO, :] * (t > 0).astype(F32)
    nxt = next_ref[...].astype(F32)[0:half, :] * (t < nt - 1).astype(F32)
    e = jnp.concatenate([prev, cur_ref[...].astype(F32), nxt], axis=0)
    rows = ts + HALO
    pad = MLSTM_CONV_W // 2
    acc = None
    for j in range(MLSTM_CONV_W):
        shifted = e if j == pad else pltpu.roll(e, (pad - j) % rows, 0)
        term = shifted[half:half + ts, :] * w_ref[j:j + 1, :]
        acc = term if acc is None else acc + term
    y = acc + b_ref[...]
    y = y * _sigmoid(y)
    scale = jnp.where(pl.program_id(2) < q_blocks, q_scale, 1.0).astype(F32)
    o_ref[...] = (y * scale).astype(o_ref.dtype)


def _conv(proj, conv_w, conv_b, ts, cw, q_scale):
    bsz, s, _ = proj.shape
    c = conv_w.shape[1]
    kern = functools.partial(_conv_kernel, q_blocks=(c // 2) // cw, q_scale=q_scale)
    nh = s // HALO
    return pl.pallas_call(
        kern,
        grid=(bsz, s // ts, c // cw),
        in_specs=[pl.BlockSpec((None, HALO, cw),
                               lambda b, t, j: (b, jnp.maximum(t * (ts // HALO) - 1, 0), j)),
                  pl.BlockSpec((None, ts, cw), lambda b, t, j: (b, t, j)),
                  pl.BlockSpec((None, HALO, cw),
                               lambda b, t, j: (b, jnp.minimum((t + 1) * (ts // HALO), nh - 1), j)),
                  pl.BlockSpec((MLSTM_CONV_W, cw), lambda b, t, j: (0, j)),
                  pl.BlockSpec((1, cw), lambda b, t, j: (0, j))],
        out_specs=pl.BlockSpec((None, ts, cw), lambda b, t, j: (b, t, j)),
        out_shape=jax.ShapeDtypeStruct((bsz, s, c), BF16),
        compiler_params=_params("parallel", "parallel", "parallel"),
        name="conv",
    )(proj, proj, proj, conv_w, conv_b.reshape(1, c))


def _mlstm_kernel(qkf_ref, qkb_ref, vf_ref, vb_ref, gf_ref, gb_ref, hf_ref, hb_ref,
                  c_scr, n_scr, m_scr):
    ln = vf_ref.shape[0]
    inner = vf_ref.shape[1]
    dh = inner // MLSTM_HEADS

    @pl.when(pl.program_id(1) == 0)
    def _():
        c_scr[...] = jnp.zeros_like(c_scr)
        n_scr[...] = jnp.zeros_like(n_scr)
        m_scr[...] = jnp.full_like(m_scr, -jnp.inf)

    row = lax.broadcasted_iota(jnp.int32, (ln, ln), 0)
    col = lax.broadcasted_iota(jnp.int32, (ln, ln), 1)
    lower = col <= row
    upper = col >= row

    def gate_terms(g_ref, tri):
        gi = g_ref[:, 0:LANES]
        gf = g_ref[:, LANES:2 * LANES]
        a = jnp.minimum(gf, 0.0) - jnp.log(1.0 + jnp.exp(-jnp.abs(gf)))
        b = jnp.dot(tri.astype(F32), a, preferred_element_type=F32, precision=lax.Precision.HIGHEST)
        u = gi - b
        return b, u, u.T

    b_f, u_f, ut_f = gate_terms(gf_ref, lower)
    b_b, u_b, ut_b = gate_terms(gb_ref, upper)

    for direction in range(2):
        qk_ref, v_ref, h_ref = ((qkf_ref, vf_ref, hf_ref), (qkb_ref, vb_ref, hb_ref))[direction]
        bmat, umat, utmat = ((b_f, u_f, ut_f), (b_b, u_b, ut_b))[direction]
        mask = (lower, upper)[direction]
        for head in range(MLSTM_HEADS):
            j = direction * MLSTM_HEADS + head
            q = qk_ref[:, head * dh:(head + 1) * dh]
            k = qk_ref[:, inner + head * dh:inner + (head + 1) * dh]
            v = v_ref[:, head * dh:(head + 1) * dh]
            b_col = bmat[:, j:j + 1]
            u_col = umat[:, j:j + 1]
            u_row = utmat[j:j + 1, :]
            m_prev = m_scr[j:j + 1, 0:1]
            c_old = c_scr[j]
            n_old = n_scr[j:j + 1, :]

            u_mat = jnp.where(mask, u_row, -jnp.inf)
            g_col = jnp.maximum(jnp.max(u_mat, axis=1, keepdims=True), m_prev)
            decay_mat = jnp.exp(u_mat - g_col)
            w = _dot_nt(q, k) * decay_mat
            inter = jnp.exp(m_prev - g_col)
            num = _dot(w.astype(BF16), v) + inter * _dot(q, c_old.astype(BF16))
            den = (jnp.sum(w, axis=1, keepdims=True)
                   + inter * jnp.sum(q.astype(F32) * n_old, axis=1, keepdims=True))
            floor = jnp.exp(-(g_col + b_col))
            h_ref[:, head * dh:(head + 1) * dh] = num / jnp.maximum(jnp.abs(den), floor)

            g_end = jnp.max(g_col, axis=0, keepdims=True)
            b_tot = jnp.min(b_col, axis=0, keepdims=True)
            carry = jnp.exp(m_prev - g_end)
            kw = k.astype(F32) * jnp.exp(u_col - g_end)
            c_scr[j] = carry * c_old + _dot_tn(kw.astype(BF16), v)
            n_scr[j:j + 1, :] = carry * n_old + jnp.sum(kw, axis=0, keepdims=True)
            m_scr[j:j + 1, :] = jnp.broadcast_to(b_tot + g_end, (1, m_scr.shape[1]))


def _mlstm(qk, proj, gates, v_block, ln):
    bsz, s, two_inner = qk.shape
    inner = two_inner // 2
    nc = s // ln
    ng = gates.shape[-1]
    dh = inner // MLSTM_HEADS
    nchain = 2 * MLSTM_HEADS
    fwd = lambda b, i: (b, i, 0)
    bwd = lambda b, i: (b, nc - 1 - i, 0)
    return pl.pallas_call(
        _mlstm_kernel,
        grid=(bsz, nc),
        in_specs=[pl.BlockSpec((None, ln, two_inner), fwd),
                  pl.BlockSpec((None, ln, two_inner), bwd),
                  pl.BlockSpec((None, ln, inner), lambda b, i: (b, i, v_block)),
                  pl.BlockSpec((None, ln, inner), lambda b, i: (b, nc - 1 - i, v_block)),
                  pl.BlockSpec((None, ln, ng), fwd),
                  pl.BlockSpec((None, ln, ng), bwd)],
        out_specs=[pl.BlockSpec((None, ln, inner), fwd),
                   pl.BlockSpec((None, ln, inner), bwd)],
        out_shape=[jax.ShapeDtypeStruct((bsz, s, inner), F32),
                   jax.ShapeDtypeStruct((bsz, s, inner), F32)],
        scratch_shapes=[pltpu.VMEM((nchain, dh, dh), F32),
                        pltpu.VMEM((nchain, dh), F32),
                        pltpu.VMEM((nchain, LANES), F32)],
        compiler_params=_params("parallel", "arbitrary"),
        name="mlstm",
    )(qk, qk, proj, proj, gates, gates)


def _rope_tab_kernel(freq_ref, cos_ref, sin_ref):
    ts = cos_ref.shape[0]
    t = pl.program_id(0) * ts + lax.broadcasted_iota(jnp.int32, (ts, LANES), 0)
    lane = lax.broadcasted_iota(jnp.int32, (ts, LANES), 1)
    grid_shift = GRID_W.bit_length() - 1
    pos = jnp.where(lane >= LANES // 2, t & (GRID_W - 1), t >> grid_shift).astype(F32)
    ang = pos * freq_ref[...]
    sign = jnp.where((lane & (LANES // 4)) == 0, -1.0, 1.0)
    cos_ref[...] = jnp.cos(ang)
    sin_ref[...] = jnp.sin(ang) * sign


def _rope_tables(s):
    n_freq = ATTN_HEAD_DIM // 4
    freqs = ROPE_THETA ** (-jnp.arange(n_freq, dtype=F32) / n_freq)
    freq_lanes = jnp.tile(freqs, 4).reshape(1, ATTN_HEAD_DIM)
    ts = min(s, 1024)
    return pl.pallas_call(
        _rope_tab_kernel,
        grid=(s // ts,),
        in_specs=[pl.BlockSpec((1, LANES), lambda t: (0, 0))],
        out_specs=[pl.BlockSpec((ts, LANES), lambda t: (t, 0)),
                   pl.BlockSpec((ts, LANES), lambda t: (t, 0))],
        out_shape=[jax.ShapeDtypeStruct((s, LANES), F32),
                   jax.ShapeDtypeStruct((s, LANES), F32)],
        compiler_params=_params("parallel"),
        name="rope_tab",
    )(freq_lanes)


def _norm_rope(x, w, cos, sin, lane):
    ms = jnp.mean(x * x, axis=-1, keepdims=True)
    y = x * lax.rsqrt(ms + EPS) * w
    quarter = LANES // 4
    swapped = jnp.where((lane & quarter) == 0,
                        pltpu.roll(y, LANES - quarter, 1), pltpu.roll(y, quarter, 1))
    return y * cos + swapped * sin


def _qk_prep_kernel(q_ref, k_ref, v_ref, cos_ref, sin_ref, qw_ref, kw_ref, qo_ref, ko_ref, vt_ref,
                    *, q_scale):
    ts = q_ref.shape[0]
    vt_ref[...] = v_ref[...].T
    cos = cos_ref[...]
    sin = sin_ref[...]
    lane = lax.broadcasted_iota(jnp.int32, (ts, LANES), 1)
    for h in range(q_ref.shape[1] // LANES):
        sl = slice(h * LANES, (h + 1) * LANES)
        y = _norm_rope(q_ref[:, sl].astype(F32), qw_ref[...], cos, sin, lane)
        qo_ref[:, sl] = (y * q_scale).astype(qo_ref.dtype)
    for h in range(k_ref.shape[1] // LANES):
        sl = slice(h * LANES, (h + 1) * LANES)
        y = _norm_rope(k_ref[:, sl].astype(F32), kw_ref[...], cos, sin, lane)
        ko_ref[:, sl] = y.astype(ko_ref.dtype)


def _qk_prep(proj, cos, sin, qn_w, kn_w, q_block, k_block, v_block, q_inner, kv_inner, ts, q_scale):
    bsz, s, _ = proj.shape
    kern = functools.partial(_qk_prep_kernel, q_scale=q_scale)
    return pl.pallas_call(
        kern,
        grid=(bsz, s // ts),
        in_specs=[pl.BlockSpec((None, ts, q_inner), lambda b, t: (b, t, q_block)),
                  pl.BlockSpec((None, ts, kv_inner), lambda b, t: (b, t, k_block)),
                  pl.BlockSpec((None, ts, kv_inner), lambda b, t: (b, t, v_block)),
                  pl.BlockSpec((ts, LANES), lambda b, t: (t, 0)),
                  pl.BlockSpec((ts, LANES), lambda b, t: (t, 0)),
                  pl.BlockSpec((1, LANES), lambda b, t: (0, 0)),
                  pl.BlockSpec((1, LANES), lambda b, t: (0, 0))],
        out_specs=[pl.BlockSpec((None, ts, q_inner), lambda b, t: (b, t, 0)),
                   pl.BlockSpec((None, ts, kv_inner), lambda b, t: (b, t, 0)),
                   pl.BlockSpec((None, None, kv_inner, ts), lambda b, t: (b, t, 0, 0))],
        out_shape=[jax.ShapeDtypeStruct((bsz, s, q_inner), BF16),
                   jax.ShapeDtypeStruct((bsz, s, kv_inner), BF16),
                   jax.ShapeDtypeStruct((bsz, s // ts, kv_inner, ts), BF16)],
        compiler_params=_params("parallel", "parallel"),
        name="qk_prep",
    )(proj, proj, proj, cos, sin, qn_w.reshape(1, LANES), kn_w.reshape(1, LANES))


def _attn_kernel(q_ref, k_ref, vt_ref, o_ref, qs_scr, m_scr, l_scr, alpha_scr, acc_scr, st_scr, p_scr):
    tq = q_ref.shape[0]
    group = q_ref.shape[1] // LANES
    nk, tk = k_ref.shape[0], k_ref.shape[1]
    strip = 64

    for h in range(group):
        qs_scr[h * tq:(h + 1) * tq, :] = q_ref[:, h * LANES:(h + 1) * LANES]
    m_scr[...] = jnp.full_like(m_scr, -jnp.inf)
    l_scr[...] = jnp.zeros_like(l_scr)
    acc_scr[...] = jnp.zeros_like(acc_scr)

    def scores(j, slot):
        st_scr[slot] = _dot_nt(k_ref[j], qs_scr[...])

    def softmax(slot):
        for h in range(group):
            cols = slice(h * tq, (h + 1) * tq)
            m_old = m_scr[:, cols]
            cmax = jnp.full((8, tq), -jnp.inf, F32)
            for r in range(0, tk, strip):
                cmax = jnp.maximum(cmax, jnp.max(st_scr[slot, r:r + strip, cols].reshape(strip // 8, 8, tq),
                                                 axis=0))
            m_new = jnp.maximum(m_old, jnp.max(cmax, axis=0, keepdims=True))
            alpha = jnp.exp2(m_old - m_new)
            psum = jnp.zeros((8, tq), F32)
            for r in range(0, tk, strip):
                p = jnp.exp2(st_scr[slot, r:r + strip, cols] - m_new)
                psum = psum + jnp.sum(p.reshape(strip // 8, 8, tq), axis=0)
                p_scr[slot, r:r + strip, cols] = p.astype(BF16)
            l_scr[:, cols] = alpha * l_scr[:, cols] + jnp.sum(psum, axis=0, keepdims=True)
            alpha_scr[slot, :, cols] = alpha
            m_scr[:, cols] = m_new

    def pv(j, slot):
        acc_scr[...] = alpha_scr[slot] * acc_scr[...] + _dot(vt_ref[j], p_scr[slot])

    assert nk % 2 == 0
    scores(0, 0)
    scores(1, 1)
    softmax(0)

    def body(i, carry):
        j = 2 * i + 1
        scores(j + 1, 0)
        softmax(1)
        pv(j - 1, 0)
        scores(j + 2, 1)
        softmax(0)
        pv(j, 1)
        return carry

    lax.fori_loop(0, (nk - 2) // 2, body, 0)
    softmax(1)
    pv(nk - 2, 0)
    pv(nk - 1, 1)
    for h in range(group):
        cols = slice(h * tq, (h + 1) * tq)
        o_ref[:, h * LANES:(h + 1) * LANES] = (acc_scr[:, cols] / l_scr[:, cols]).T.astype(o_ref.dtype)


def _attention(q, k, vt, tq):
    bsz, s, q_inner = q.shape
    dh = ATTN_HEAD_DIM
    nk, tk = vt.shape[1], vt.shape[3]
    kvh = k.shape[-1] // dh
    gw = q_inner // kvh
    group = gw // dh
    k4 = k.reshape(bsz, nk, tk, kvh * dh)
    return pl.pallas_call(
        _attn_kernel,
        grid=(bsz, kvh, s // tq),
        in_specs=[pl.BlockSpec((None, tq, gw), lambda b, g, i: (b, i, g)),
                  pl.BlockSpec((None, nk, tk, dh), lambda b, g, i: (b, 0, 0, g)),
                  pl.BlockSpec((None, nk, dh, tk), lambda b, g, i: (b, 0, g, 0))],
        out_specs=pl.BlockSpec((None, tq, gw), lambda b, g, i: (b, i, g)),
        out_shape=jax.ShapeDtypeStruct((bsz, s, q_inner), BF16),
        scratch_shapes=[pltpu.VMEM((group * tq, dh), BF16),
                        pltpu.VMEM((1, group * tq), F32),
                        pltpu.VMEM((1, group * tq), F32),
                        pltpu.VMEM((2, 1, group * tq), F32),
                        pltpu.VMEM((dh, group * tq), F32),
                        pltpu.VMEM((2, tk, group * tq), F32),
                        pltpu.VMEM((2, tk, group * tq), BF16)],
        compiler_params=_params("parallel", "parallel", "parallel"),
        name="attn",
    )(q, k4, vt)


def _merge_kernel(hf_ref, hb_ref, o_ref, bm_ref, ba_ref, ya_ref, x_ref, mod_ref, gn_ref, nw_ref,
                  wm_ref, wa_ref, wo_ref, out_ref, ym_scr):
    d = x_ref.shape[-1]
    dh = d // MLSTM_HEADS
    for head in range(MLSTM_HEADS):
        sl = slice(head * dh, (head + 1) * dh)
        h = hf_ref[:, sl] + hb_ref[:, sl]
        mu = jnp.mean(h, axis=-1, keepdims=True)
        hc = h - mu
        var = jnp.mean(hc * hc, axis=-1, keepdims=True)
        hn = hc * lax.rsqrt(var + EPS) * gn_ref[:, sl]
        ym_scr[:, sl] = (hn * _sigmoid(o_ref[:, sl].astype(F32))).astype(BF16)
    pm = _dot(ym_scr[...], wm_ref[...])
    pa = _dot(ya_ref[...], wa_ref[...])
    y = _sigmoid(bm_ref[...].astype(F32)) * pm + _sigmoid(ba_ref[...].astype(F32)) * pa
    y2 = _dot(y.astype(BF16), wo_ref[...])
    ms = jnp.mean(y2 * y2, axis=-1, keepdims=True)
    gate = mod_ref[:, 2 * d:3 * d]
    out_ref[...] = x_ref[...] + gate * (y2 * lax.rsqrt(ms + EPS) * nw_ref[...])


def _merge(hf, hb, proj, ya, x, mod3, gn_w, norm_w, wm, wa, wo, o_block, bm_block, ba_block, tm):
    bsz, s, d = x.shape
    tok = lambda b, t: (b, t, 0)
    const = lambda b, t: (0, 0)
    return pl.pallas_call(
        _merge_kernel,
        grid=(bsz, s // tm),
        in_specs=[pl.BlockSpec((None, tm, d), tok),
                  pl.BlockSpec((None, tm, d), tok),
                  pl.BlockSpec((None, tm, d), lambda b, t: (b, t, o_block)),
                  pl.BlockSpec((None, tm, d), lambda b, t: (b, t, bm_block)),
                  pl.BlockSpec((None, tm, d), lambda b, t: (b, t, ba_block)),
                  pl.BlockSpec((None, tm, d), tok),
                  pl.BlockSpec((None, tm, d), tok),
                  pl.BlockSpec((None, 1, mod3.shape[-1]), lambda b, t: (b, 0, 0)),
                  pl.BlockSpec((1, d), const),
                  pl.BlockSpec((1, d), const),
                  pl.BlockSpec((d, d), const),
                  pl.BlockSpec((d, d), const),
                  pl.BlockSpec((d, d), const)],
        out_specs=pl.BlockSpec((None, tm, d), tok),
        out_shape=jax.ShapeDtypeStruct((bsz, s, d), F32),
        scratch_shapes=[pltpu.VMEM((tm, d), BF16)],
        compiler_params=_params("parallel", "parallel"),
        name="merge",
    )(hf, hb, proj, proj, proj, ya, x, mod3, gn_w, norm_w, wm, wa, wo)


def _mlp_kernel(x_ref, mod_ref, n1_ref, n2_ref, w1_ref, w2_ref, out_ref, *, ff_chunk):
    d = x_ref.shape[-1]
    x = x_ref[...]
    ms = jnp.mean(x * x, axis=-1, keepdims=True)
    shift = mod_ref[:, 3 * d:4 * d]
    scale = mod_ref[:, 4 * d:5 * d]
    gate = mod_ref[:, 5 * d:6 * d]
    h = (x * lax.rsqrt(ms + EPS) * n1_ref[...] * (1.0 + scale) + shift).astype(BF16)
    acc = None
    for c in range(w1_ref.shape[1] // ff_chunk):
        sl = slice(c * ff_chunk, (c + 1) * ff_chunk)
        u = jnp.maximum(_dot(h, w1_ref[:, sl]), 0.0)
        part = _dot((u * u).astype(BF16), w2_ref[sl, :])
        acc = part if acc is None else acc + part
    ms2 = jnp.mean(acc * acc, axis=-1, keepdims=True)
    out_ref[...] = x + gate * (acc * lax.rsqrt(ms2 + EPS) * n2_ref[...])


def _mlp(x, mod3, n1, n2, w1, w2, tm, ff_chunk):
    bsz, s, d = x.shape
    ff = w1.shape[1]
    tok = lambda b, t: (b, t, 0)
    const = lambda b, t: (0, 0)
    kern = functools.partial(_mlp_kernel, ff_chunk=ff_chunk)
    return pl.pallas_call(
        kern,
        grid=(bsz, s // tm),
        in_specs=[pl.BlockSpec((None, tm, d), tok),
                  pl.BlockSpec((None, 1, mod3.shape[-1]), lambda b, t: (b, 0, 0)),
                  pl.BlockSpec((1, d), const),
                  pl.BlockSpec((1, d), const),
                  pl.BlockSpec((d, ff), const, pipeline_mode=pl.Buffered(1)),
                  pl.BlockSpec((ff, d), const, pipeline_mode=pl.Buffered(1))],
        out_specs=pl.BlockSpec((None, tm, d), tok),
        out_shape=jax.ShapeDtypeStruct((bsz, s, d), F32),
        compiler_params=_params("parallel", "parallel"),
        name="mlp",
    )(x, mod3, n1, n2, w1, w2)


def _layer(x, mod, norm1_pre, norm1_post, w_in, b_gates, conv_w, conv_b, mlstm_gn, attn_qnorm,
           attn_knorm, w_branch_m, w_branch_a, w_out, norm2_pre, norm2_post, w_mlp_in, w_mlp_out):
    bsz, s, d = x.shape
    inner = d
    n_gate = 4 * MLSTM_HEADS
    q_inner = d
    kv_inner = ATTN_KV_HEADS * ATTN_HEAD_DIM

    o_qk, o_v, o_o = 0, 2 * inner, 3 * inner
    o_g = 4 * inner
    o_qa = o_g + n_gate
    o_ka = o_qa + q_inner
    o_va = o_ka + kv_inner
    o_br = o_va + kv_inner
    w_main = jnp.concatenate([w_in[:, o_qk:o_g], w_in[:, o_qa:o_ka], w_in[:, o_br:o_br + 2 * d],
                              w_in[:, o_ka:o_va], w_in[:, o_va:o_br]], axis=1).astype(BF16)
    c_v, c_o, c_qa, c_bm, c_ba = 2 * inner, 3 * inner, 4 * inner, 5 * inner, 6 * inner
    c_ka = 7 * inner
    c_va = c_ka + kv_inner
    hm = MLSTM_HEADS
    wg = w_in[:, o_g:o_qa]
    gate_cols = lambda a: (jnp.concatenate([a[..., 0:hm], a[..., 2 * hm:3 * hm]], -1),
                           jnp.concatenate([a[..., hm:2 * hm], a[..., 3 * hm:4 * hm]], -1))
    wi, wf = gate_cols(wg)
    bi, bf = gate_cols(b_gates)
    lane_pad = lambda a: jnp.pad(a, [(0, 0)] * (a.ndim - 1) + [(0, LANES - a.shape[-1])])
    w_gates = jnp.concatenate([lane_pad(wi), lane_pad(wf)], -1).astype(BF16)
    bias_gates = jnp.concatenate([lane_pad(bi), lane_pad(bf)], -1).reshape(1, 2 * LANES)

    mod3 = mod.reshape(bsz, 1, mod.shape[-1])
    tm = min(s, 1024)
    proj, gates = _in_proj(x, mod3, norm1_pre.reshape(1, d), w_main, w_gates, bias_gates,
                           tm=tm, tn=1536)

    dh_m = inner // MLSTM_HEADS
    qk = _conv(proj, conv_w, conv_b, ts=min(s, 512), cw=512, q_scale=dh_m ** -0.5)
    h_f, h_b = _mlstm(qk, proj, gates, v_block=c_v // inner, ln=min(s, 256))

    cos, sin = _rope_tables(s)
    q_rot, k_rot, v_t = _qk_prep(proj, cos, sin, attn_qnorm, attn_knorm, q_block=c_qa // q_inner,
                                 k_block=c_ka // kv_inner, v_block=c_va // kv_inner,
                                 q_inner=q_inner, kv_inner=kv_inner, ts=min(s, 512),
                                 q_scale=ATTN_HEAD_DIM ** -0.5 * math.log2(math.e))
    y_a = _attention(q_rot, k_rot, v_t, tq=min(s, 256))

    x1 = _merge(h_f, h_b, proj, y_a, x, mod3, mlstm_gn.reshape(1, d), norm1_post.reshape(1, d),
                w_branch_m.astype(BF16), w_branch_a.astype(BF16), w_out.astype(BF16),
                o_block=c_o // d, bm_block=c_bm // d, ba_block=c_ba // d, tm=min(s, 512))
    return _mlp(x1, mod3, norm2_pre.reshape(1, d), norm2_post.reshape(1, d),
                w_mlp_in.astype(BF16), w_mlp_out.astype(BF16), tm=min(s, 512), ff_chunk=1024)


def kernel(x, c, w_ada, b_ada, norm1_pre, norm1_post, w_in, b_gates, conv_w, conv_b, mlstm_gn, attn_qnorm, attn_knorm, w_branch_m, w_branch_a, w_out, norm2_pre, norm2_post, w_mlp_in, w_mlp_out):
    for l in range(w_ada.shape[0]):
        mod = _ada(c, w_ada[l], b_ada[l])
        x = _layer(x, mod, norm1_pre[l], norm1_post[l], w_in[l], b_gates[l], conv_w[l], conv_b[l],
                   mlstm_gn[l], attn_qnorm[l], attn_knorm[l], w_branch_m[l], w_branch_a[l], w_out[l],
                   norm2_pre[l], norm2_post[l], w_mlp_in[l], w_mlp_out[l])
    return x
```

```python
import functools
import math

import jax
import jax.numpy as jnp
from jax import lax
from jax.experimental import pallas as pl
from jax.experimental.pallas import tpu as pltpu

EPS = 1e-6
GRID_W = 64
ROPE_THETA = 10000.0
MLSTM_HEADS = 4
MLSTM_CONV_W = 5
ATTN_HEAD_DIM = 128
ATTN_KV_HEADS = 2
LANES = 128
HALO = 16
ONES_ROWS = 16

F32 = jnp.float32
BF16 = jnp.bfloat16
VMEM_LIMIT = 56 * 1024 * 1024


def _params(*sem):
    return pltpu.CompilerParams(dimension_semantics=sem, vmem_limit_bytes=VMEM_LIMIT)


def _sigmoid(x):
    return 1.0 / (1.0 + jnp.exp(-x))


def _dot(a, b):
    return jnp.dot(a, b, preferred_element_type=F32)


def _dot_nt(a, b):
    return lax.dot_general(a, b, (((1,), (1,)), ((), ())), preferred_element_type=F32)


def _dot_tn(a, b):
    return lax.dot_general(a, b, (((0,), (0,)), ((), ())), preferred_element_type=F32)


def _ada_kernel(c_ref, w_ref, b_ref, o_ref):
    c = c_ref[...]
    sc = c * _sigmoid(c)
    o_ref[...] = jnp.dot(sc, w_ref[...], preferred_element_type=F32,
                         precision=lax.Precision.HIGHEST) + b_ref[...]


def _ada(c, w, b):
    bsz, d = c.shape
    n = w.shape[1]
    tn = 1024
    return pl.pallas_call(
        _ada_kernel,
        grid=(n // tn,),
        in_specs=[pl.BlockSpec((bsz, d), lambda j: (0, 0)),
                  pl.BlockSpec((d, tn), lambda j: (0, j)),
                  pl.BlockSpec((1, tn), lambda j: (0, j))],
        out_specs=pl.BlockSpec((bsz, tn), lambda j: (0, j)),
        out_shape=jax.ShapeDtypeStruct((bsz, n), F32),
        compiler_params=_params("parallel"),
        name="ada",
    )(c, w, b.reshape(1, n))


def _rope_head_perm():
    quarter = ATTN_HEAD_DIM // 4
    blocks = [0, 2, 1, 3]
    return jnp.concatenate([jnp.arange(b * quarter, (b + 1) * quarter) for b in blocks])


def _norm_rope(x, w, cos, sin):
    ms = jnp.mean(x * x, axis=-1, keepdims=True)
    y = x * lax.rsqrt(ms + EPS) * w
    return y * cos + pltpu.roll(y, LANES // 2, 1) * sin


def _in_proj_kernel(xp_ref, x_ref, xn_ref, mod_ref, nw_ref, w_ref, wg_ref, bg_ref, cw_ref, cb_ref,
                    cos_ref, sin_ref, qw_ref, kw_ref,
                    qk_ref, mid_ref, qo_ref, ko_ref, vt_ref, g_ref, h_scr, e_scr,
                    *, chunk, q_scale_m, q_scale_a):
    t = pl.program_id(1)
    nt = pl.num_programs(1)
    tm, d = x_ref.shape
    conv_cols = qk_ref.shape[1]
    mid_cols = mid_ref.shape[1]
    qa_cols = qo_ref.shape[1]
    ka_cols = ko_ref.shape[1]
    gain = nw_ref[...] * (1.0 + mod_ref[:, d:2 * d])
    shift = mod_ref[:, 0:d]

    def normed(xv):
        ms = jnp.mean(xv * xv, axis=-1, keepdims=True)
        return (xv * lax.rsqrt(ms + EPS) * gain + shift).astype(BF16)

    h_scr[0:HALO, :] = normed(xp_ref[...])
    h_scr[HALO:HALO + tm, :] = normed(x_ref[...])
    h_scr[HALO + tm:, :] = normed(xn_ref[...])
    centre = slice(HALO, HALO + tm)
    g_ref[...] = _dot(h_scr[centre, :], wg_ref[...]) + bg_ref[...]

    half = HALO // 2
    pad = MLSTM_CONV_W // 2
    has_prev = (t > 0).astype(F32)
    has_next = (t < nt - 1).astype(F32)

    def conv_chunk(c0):
        cs = slice(c0, c0 + chunk)
        e = _dot(h_scr[...], w_ref[:, cs])
        e_scr[0:half, :] = e[half:HALO] * has_prev
        e_scr[half:half + tm, :] = e[HALO:HALO + tm]
        e_scr[half + tm:, :] = e[HALO + tm:HALO + tm + half] * has_next
        acc = None
        for j in range(MLSTM_CONV_W):
            term = e_scr[half - pad + j:half - pad + j + tm, :] * cw_ref[j:j + 1, cs]
            acc = term if acc is None else acc + term
        y = acc + cb_ref[:, cs]
        y = y * _sigmoid(y)
        if c0 < conv_cols // 2:
            y = y * q_scale_m
        qk_ref[:, cs] = y.astype(qk_ref.dtype)

    mid_base = conv_cols

    def mid_chunk(c0):
        mid_ref[:, c0:c0 + chunk] = _dot(h_scr[centre, :],
                                         w_ref[:, mid_base + c0:mid_base + c0 + chunk]).astype(mid_ref.dtype)

    qa_base = mid_base + mid_cols
    cos = cos_ref[...]
    sin = sin_ref[...]

    def q_chunk(c0):
        a = _dot(h_scr[centre, :], w_ref[:, qa_base + c0:qa_base + c0 + chunk])
        for hh in range(chunk // LANES):
            y = _norm_rope(a[:, hh * LANES:(hh + 1) * LANES], qw_ref[...], cos, sin)
            qo_ref[:, c0 + hh * LANES:c0 + (hh + 1) * LANES] = (y * q_scale_a).astype(qo_ref.dtype)

    kv_base = qa_base + qa_cols

    def kv_chunk():
        a = _dot(h_scr[centre, :], w_ref[:, kv_base:kv_base + 2 * ka_cols])
        for g in range(ka_cols // LANES):
            y = _norm_rope(a[:, g * LANES:(g + 1) * LANES], kw_ref[...], cos, sin)
            ko_ref[:, g * LANES:(g + 1) * LANES] = y.astype(ko_ref.dtype)
            r0 = g * (LANES + ONES_ROWS)
            v = a[:, ka_cols + g * LANES:ka_cols + (g + 1) * LANES].astype(vt_ref.dtype)
            vt_ref[r0:r0 + LANES, :] = v.T
            vt_ref[r0 + LANES:r0 + LANES + ONES_ROWS, :] = jnp.ones((ONES_ROWS, tm), vt_ref.dtype)

    heavy = ([functools.partial(conv_chunk, c0) for c0 in range(0, conv_cols, chunk)]
             + [functools.partial(q_chunk, c0) for c0 in range(0, qa_cols, chunk)] + [kv_chunk])
    light = [functools.partial(mid_chunk, c0) for c0 in range(0, mid_cols, chunk)]
    for i in range(max(len(heavy), len(light))):
        for task in heavy[i:i + 1] + light[i:i + 1]:
            task()


def _in_proj(x, mod3, norm_w, w_main, w_gates, b_gates, conv_w, conv_b, cos, sin, qn_w, kn_w,
             conv_cols, mid_cols, qa_cols, ka_cols, tm, q_scale_m, q_scale_a):
    bsz, s, d = x.shape
    n = w_main.shape[1]
    assert n == conv_cols + mid_cols + qa_cols + 2 * ka_cols
    ng = w_gates.shape[1]
    vt_rows = (ka_cols // LANES) * (LANES + ONES_ROWS)
    nh = s // HALO
    chunk = 512
    kern = functools.partial(_in_proj_kernel, chunk=chunk, q_scale_m=q_scale_m, q_scale_a=q_scale_a)
    tok = lambda b, t: (b, t, 0)
    const = lambda b, t: (0, 0)
    bf = lambda cols: jax.ShapeDtypeStruct((bsz, s, cols), BF16)
    return pl.pallas_call(
        kern,
        grid=(bsz, s // tm),
        in_specs=[pl.BlockSpec((None, HALO, d), lambda b, t: (b, jnp.maximum(t * (tm // HALO) - 1, 0), 0)),
                  pl.BlockSpec((None, tm, d), tok),
                  pl.BlockSpec((None, HALO, d),
                               lambda b, t: (b, jnp.minimum((t + 1) * (tm // HALO), nh - 1), 0)),
                  pl.BlockSpec((None, 1, mod3.shape[-1]), lambda b, t: (b, 0, 0)),
                  pl.BlockSpec((1, d), const),
                  pl.BlockSpec((d, n), const, pipeline_mode=pl.Buffered(1)),
                  pl.BlockSpec((d, ng), const),
                  pl.BlockSpec((1, ng), const),
                  pl.BlockSpec((MLSTM_CONV_W, conv_cols), const),
                  pl.BlockSpec((1, conv_cols), const),
                  pl.BlockSpec((tm, LANES), lambda b, t: (t, 0)),
                  pl.BlockSpec((tm, LANES), lambda b, t: (t, 0)),
                  pl.BlockSpec((1, LANES), const),
                  pl.BlockSpec((1, LANES), const)],
        out_specs=[pl.BlockSpec((None, tm, conv_cols), tok),
                   pl.BlockSpec((None, tm, mid_cols), tok),
                   pl.BlockSpec((None, tm, qa_cols), tok),
                   pl.BlockSpec((None, tm, ka_cols), tok),
                   pl.BlockSpec((None, None, vt_rows, tm), lambda b, t: (b, t, 0, 0)),
                   pl.BlockSpec((None, tm, ng), tok)],
        out_shape=[bf(conv_cols), bf(mid_cols), bf(qa_cols), bf(ka_cols),
                   jax.ShapeDtypeStruct((bsz, s // tm, vt_rows, tm), BF16),
                   jax.ShapeDtypeStruct((bsz, s, ng), F32)],
        scratch_shapes=[pltpu.VMEM((tm + 2 * HALO, d), BF16),
                        pltpu.VMEM((tm + HALO, chunk), F32)],
        compiler_params=_params("parallel", "parallel"),
        name="in_proj",
    )(x, x, x, mod3, norm_w, w_main, w_gates, b_gates, conv_w, conv_b.reshape(1, conv_cols), cos, sin,
      qn_w.reshape(1, LANES), kn_w.reshape(1, LANES))


def _mlstm_kernel(qkf_ref, qkb_ref, vf_ref, vb_ref, gf_ref, gb_ref, hf_ref, hb_ref,
                  c_scr, n_scr, m_scr):
    ln = vf_ref.shape[0]
    inner = vf_ref.shape[1]
    dh = inner // MLSTM_HEADS

    @pl.when(pl.program_id(1) == 0)
    def _():
        c_scr[...] = jnp.zeros_like(c_scr)
        n_scr[...] = jnp.zeros_like(n_scr)
        m_scr[...] = jnp.full_like(m_scr, -jnp.inf)

    row = lax.broadcasted_iota(jnp.int32, (ln, ln), 0)
    col = lax.broadcasted_iota(jnp.int32, (ln, ln), 1)
    lower = col <= row
    upper = col >= row

    def gate_terms(g_ref, tri):
        gi = g_ref[:, 0:LANES]
        gf = g_ref[:, LANES:2 * LANES]
        a = jnp.minimum(gf, 0.0) - jnp.log(1.0 + jnp.exp(-jnp.abs(gf)))
        b = jnp.dot(tri.astype(F32), a, preferred_element_type=F32, precision=lax.Precision.HIGHEST)
        u = gi - b
        return b, u, u.T

    b_f, u_f, ut_f = gate_terms(gf_ref, lower)
    b_b, u_b, ut_b = gate_terms(gb_ref, upper)

    for direction in range(2):
        qk_ref, v_ref, h_ref = ((qkf_ref, vf_ref, hf_ref), (qkb_ref, vb_ref, hb_ref))[direction]
        bmat, umat, utmat = ((b_f, u_f, ut_f), (b_b, u_b, ut_b))[direction]
        mask = (lower, upper)[direction]
        for head in range(MLSTM_HEADS):
            j = direction * MLSTM_HEADS + head
            q = qk_ref[:, head * dh:(head + 1) * dh]
            k = qk_ref[:, inner + head * dh:inner + (head + 1) * dh]
            v = v_ref[:, head * dh:(head + 1) * dh]
            b_col = bmat[:, j:j + 1]
            u_col = umat[:, j:j + 1]
            u_row = utmat[j:j + 1, :]
            m_prev = m_scr[j:j + 1, 0:1]
            c_old = c_scr[j]
            n_old = n_scr[j:j + 1, :]

            u_mat = jnp.where(mask, u_row, -jnp.inf)
            g_col = jnp.maximum(jnp.max(u_mat, axis=1, keepdims=True), m_prev)
            decay_mat = jnp.exp(u_mat - g_col)
            w = _dot_nt(q, k) * decay_mat
            inter = jnp.exp(m_prev - g_col)
            num = _dot(w.astype(BF16), v) + inter * _dot(q, c_old.astype(BF16))
            den = (jnp.sum(w, axis=1, keepdims=True)
                   + inter * jnp.sum(q.astype(F32) * n_old, axis=1, keepdims=True))
            floor = jnp.exp(-(g_col + b_col))
            h_ref[:, head * dh:(head + 1) * dh] = num / jnp.maximum(jnp.abs(den), floor)

            g_end = jnp.max(g_col, axis=0, keepdims=True)
            b_tot = jnp.min(b_col, axis=0, keepdims=True)
            carry = jnp.exp(m_prev - g_end)
            kw = k.astype(F32) * jnp.exp(u_col - g_end)
            c_scr[j] = carry * c_old + _dot_tn(kw.astype(BF16), v)
            n_scr[j:j + 1, :] = carry * n_old + jnp.sum(kw, axis=0, keepdims=True)
            m_scr[j:j + 1, :] = jnp.broadcast_to(b_tot + g_end, (1, m_scr.shape[1]))


def _mlstm(qk, mid, gates, v_block, ln):
    bsz, s, two_inner = qk.shape
    inner = two_inner // 2
    nc = s // ln
    ng = gates.shape[-1]
    dh = inner // MLSTM_HEADS
    nchain = 2 * MLSTM_HEADS
    fwd = lambda b, i: (b, i, 0)
    bwd = lambda b, i: (b, nc - 1 - i, 0)
    return pl.pallas_call(
        _mlstm_kernel,
        grid=(bsz, nc),
        in_specs=[pl.BlockSpec((None, ln, two_inner), fwd),
                  pl.BlockSpec((None, ln, two_inner), bwd),
                  pl.BlockSpec((None, ln, inner), lambda b, i: (b, i, v_block)),
                  pl.BlockSpec((None, ln, inner), lambda b, i: (b, nc - 1 - i, v_block)),
                  pl.BlockSpec((None, ln, ng), fwd),
                  pl.BlockSpec((None, ln, ng), bwd)],
        out_specs=[pl.BlockSpec((None, ln, inner), fwd),
                   pl.BlockSpec((None, ln, inner), bwd)],
        out_shape=[jax.ShapeDtypeStruct((bsz, s, inner), F32),
                   jax.ShapeDtypeStruct((bsz, s, inner), F32)],
        scratch_shapes=[pltpu.VMEM((nchain, dh, dh), F32),
                        pltpu.VMEM((nchain, dh), F32),
                        pltpu.VMEM((nchain, LANES), F32)],
        compiler_params=_params("parallel", "arbitrary"),
        name="mlstm",
    )(qk, qk, mid, mid, gates, gates)


def _rope_tab_kernel(freq_ref, cos_ref, sin_ref):
    ts = cos_ref.shape[0]
    t = pl.program_id(0) * ts + lax.broadcasted_iota(jnp.int32, (ts, LANES), 0)
    lane = lax.broadcasted_iota(jnp.int32, (ts, LANES), 1)
    grid_shift = GRID_W.bit_length() - 1
    is_col = (lane & (LANES // 4)) != 0
    pos = jnp.where(is_col, t & (GRID_W - 1), t >> grid_shift).astype(F32)
    ang = pos * freq_ref[...]
    sign = jnp.where(lane < LANES // 2, -1.0, 1.0)
    cos_ref[...] = jnp.cos(ang)
    sin_ref[...] = jnp.sin(ang) * sign


def _rope_tables(s):
    n_freq = ATTN_HEAD_DIM // 4
    freqs = ROPE_THETA ** (-jnp.arange(n_freq, dtype=F32) / n_freq)
    freq_lanes = jnp.tile(freqs, 4).reshape(1, ATTN_HEAD_DIM)
    ts = min(s, 1024)
    return pl.pallas_call(
        _rope_tab_kernel,
        grid=(s // ts,),
        in_specs=[pl.BlockSpec((1, LANES), lambda t: (0, 0))],
        out_specs=[pl.BlockSpec((ts, LANES), lambda t: (t, 0)),
                   pl.BlockSpec((ts, LANES), lambda t: (t, 0))],
        out_shape=[jax.ShapeDtypeStruct((s, LANES), F32),
                   jax.ShapeDtypeStruct((s, LANES), F32)],
        compiler_params=_params("parallel"),
        name="rope_tab",
    )(freq_lanes)


def _attn_kernel(q_ref, k_ref, vt_ref, o_ref, qs_scr, m_scr, alpha_scr, acc_scr, st_scr, p_scr):
    tq = q_ref.shape[0]
    group = q_ref.shape[1] // LANES
    nk, tk = k_ref.shape[0], k_ref.shape[1]
    dh = LANES
    strip = 128

    for h in range(group):
        qs_scr[h * tq:(h + 1) * tq, :] = q_ref[:, h * LANES:(h + 1) * LANES]
    m_scr[...] = jnp.full_like(m_scr, -jnp.inf)
    acc_scr[...] = jnp.zeros_like(acc_scr)

    def scores(j, slot):
        st_scr[slot] = _dot_nt(k_ref[j], qs_scr[...]).astype(BF16)

    def softmax(slot):
        for h in range(group):
            cols = slice(h * tq, (h + 1) * tq)
            m_old = m_scr[:, cols]
            cmax = jnp.max(st_scr[slot, :, cols], axis=0, keepdims=True).astype(F32)
            m_new = jnp.maximum(m_old, cmax)
            m_b = m_new.astype(BF16)
            for r in range(0, tk, strip):
                p_scr[slot, r:r + strip, cols] = jnp.exp2(st_scr[slot, r:r + strip, cols] - m_b)
            alpha_scr[slot, :, cols] = jnp.exp2(m_old - m_new)
            m_scr[:, cols] = m_new

    def pv(j, slot):
        acc_scr[...] = alpha_scr[slot] * acc_scr[...] + _dot(vt_ref[j], p_scr[slot])

    assert nk % 2 == 0
    scores(0, 0)
    scores(1, 1)
    softmax(0)

    def body(i, carry):
        j = 2 * i + 1
        scores(j + 1, 0)
        softmax(1)
        pv(j - 1, 0)
        scores(j + 2, 1)
        softmax(0)
        pv(j, 1)
        return carry

    lax.fori_loop(0, (nk - 2) // 2, body, 0)
    softmax(1)
    pv(nk - 2, 0)
    pv(nk - 1, 1)
    for h in range(group):
        cols = slice(h * tq, (h + 1) * tq)
        out_t = acc_scr[0:dh, cols] / acc_scr[dh:dh + 1, cols]
        o_ref[:, h * LANES:(h + 1) * LANES] = out_t.T.astype(o_ref.dtype)


def _attention(q, k, vt, tq):
    bsz, s, q_inner = q.shape
    dh = ATTN_HEAD_DIM
    nk, tk = vt.shape[1], vt.shape[3]
    kvh = k.shape[-1] // dh
    gw = q_inner // kvh
    group = gw // dh
    vrows = dh + ONES_ROWS
    k4 = k.reshape(bsz, nk, tk, kvh * dh)
    return pl.pallas_call(
        _attn_kernel,
        grid=(bsz, kvh, s // tq),
        in_specs=[pl.BlockSpec((None, tq, gw), lambda b, g, i: (b, i, g)),
                  pl.BlockSpec((None, nk, tk, dh), lambda b, g, i: (b, 0, 0, g)),
                  pl.BlockSpec((None, nk, vrows, tk), lambda b, g, i: (b, 0, g, 0))],
        out_specs=pl.BlockSpec((None, tq, gw), lambda b, g, i: (b, i, g)),
        out_shape=jax.ShapeDtypeStruct((bsz, s, q_inner), BF16),
        scratch_shapes=[pltpu.VMEM((group * tq, dh), BF16),
                        pltpu.VMEM((1, group * tq), F32),
                        pltpu.VMEM((2, 1, group * tq), F32),
                        pltpu.VMEM((vrows, group * tq), F32),
                        pltpu.VMEM((2, tk, group * tq), BF16),
                        pltpu.VMEM((2, tk, group * tq), BF16)],
        compiler_params=_params("parallel", "parallel", "parallel"),
        name="attn",
    )(q, k4, vt)


def _merge_kernel(hf_ref, hb_ref, o_ref, bm_ref, ba_ref, ya_ref, x_ref, mod_ref, gn_ref, nw_ref,
                  wm_ref, wa_ref, wo_ref, out_ref, ym_scr):
    d = x_ref.shape[-1]
    dh = d // MLSTM_HEADS
    for head in range(MLSTM_HEADS):
        sl = slice(head * dh, (head + 1) * dh)
        h = hf_ref[:, sl] + hb_ref[:, sl]
        mu = jnp.mean(h, axis=-1, keepdims=True)
        hc = h - mu
        var = jnp.mean(hc * hc, axis=-1, keepdims=True)
        hn = hc * lax.rsqrt(var + EPS) * gn_ref[:, sl]
        ym_scr[:, sl] = (hn * _sigmoid(o_ref[:, sl].astype(F32))).astype(BF16)
    pm = _dot(ym_scr[...], wm_ref[...])
    pa = _dot(ya_ref[...], wa_ref[...])
    y = _sigmoid(bm_ref[...].astype(F32)) * pm + _sigmoid(ba_ref[...].astype(F32)) * pa
    y2 = _dot(y.astype(BF16), wo_ref[...])
    ms = jnp.mean(y2 * y2, axis=-1, keepdims=True)
    gate = mod_ref[:, 2 * d:3 * d]
    out_ref[...] = x_ref[...] + gate * (y2 * lax.rsqrt(ms + EPS) * nw_ref[...])


def _merge(hf, hb, mid, ya, x, mod3, gn_w, norm_w, wm, wa, wo, o_block, bm_block, ba_block, tm):
    bsz, s, d = x.shape
    tok = lambda b, t: (b, t, 0)
    const = lambda b, t: (0, 0)
    return pl.pallas_call(
        _merge_kernel,
        grid=(bsz, s // tm),
        in_specs=[pl.BlockSpec((None, tm, d), tok),
                  pl.BlockSpec((None, tm, d), tok),
                  pl.BlockSpec((None, tm, d), lambda b, t: (b, t, o_block)),
                  pl.BlockSpec((None, tm, d), lambda b, t: (b, t, bm_block)),
                  pl.BlockSpec((None, tm, d), lambda b, t: (b, t, ba_block)),
                  pl.BlockSpec((None, tm, d), tok),
                  pl.BlockSpec((None, tm, d), tok),
                  pl.BlockSpec((None, 1, mod3.shape[-1]), lambda b, t: (b, 0, 0)),
                  pl.BlockSpec((1, d), const),
                  pl.BlockSpec((1, d), const),
                  pl.BlockSpec((d, d), const),
                  pl.BlockSpec((d, d), const),
                  pl.BlockSpec((d, d), const)],
        out_specs=pl.BlockSpec((None, tm, d), tok),
        out_shape=jax.ShapeDtypeStruct((bsz, s, d), F32),
        scratch_shapes=[pltpu.VMEM((tm, d), BF16)],
        compiler_params=_params("parallel", "parallel"),
        name="merge",
    )(hf, hb, mid, mid, mid, ya, x, mod3, gn_w, norm_w, wm, wa, wo)


def _mlp_kernel(x_ref, mod_ref, n1_ref, n2_ref, w1_ref, w2_ref, out_ref, *, ff_chunk):
    d = x_ref.shape[-1]
    x = x_ref[...]
    ms = jnp.mean(x * x, axis=-1, keepdims=True)
    shift = mod_ref[:, 3 * d:4 * d]
    scale = mod_ref[:, 4 * d:5 * d]
    gate = mod_ref[:, 5 * d:6 * d]
    h = (x * lax.rsqrt(ms + EPS) * n1_ref[...] * (1.0 + scale) + shift).astype(BF16)
    acc = None
    for c in range(w1_ref.shape[1] // ff_chunk):
        sl = slice(c * ff_chunk, (c + 1) * ff_chunk)
        u = jnp.maximum(_dot(h, w1_ref[:, sl]), 0.0)
        part = _dot((u * u).astype(BF16), w2_ref[sl, :])
        acc = part if acc is None else acc + part
    ms2 = jnp.mean(acc * acc, axis=-1, keepdims=True)
    out_ref[...] = x + gate * (acc * lax.rsqrt(ms2 + EPS) * n2_ref[...])


def _mlp(x, mod3, n1, n2, w1, w2, tm, ff_chunk):
    bsz, s, d = x.shape
    ff = w1.shape[1]
    tok = lambda b, t: (b, t, 0)
    const = lambda b, t: (0, 0)
    kern = functools.partial(_mlp_kernel, ff_chunk=ff_chunk)
    return pl.pallas_call(
        kern,
        grid=(bsz, s // tm),
        in_specs=[pl.BlockSpec((None, tm, d), tok),
                  pl.BlockSpec((None, 1, mod3.shape[-1]), lambda b, t: (b, 0, 0)),
                  pl.BlockSpec((1, d), const),
                  pl.BlockSpec((1, d), const),
                  pl.BlockSpec((d, ff), const, pipeline_mode=pl.Buffered(1)),
                  pl.BlockSpec((ff, d), const, pipeline_mode=pl.Buffered(1))],
        out_specs=pl.BlockSpec((None, tm, d), tok),
        out_shape=jax.ShapeDtypeStruct((bsz, s, d), F32),
        compiler_params=_params("parallel", "parallel"),
        name="mlp",
    )(x, mod3, n1, n2, w1, w2)


def _layer(x, mod, norm1_pre, norm1_post, w_in, b_gates, conv_w, conv_b, mlstm_gn, attn_qnorm,
           attn_knorm, w_branch_m, w_branch_a, w_out, norm2_pre, norm2_post, w_mlp_in, w_mlp_out):
    bsz, s, d = x.shape
    inner = d
    n_gate = 4 * MLSTM_HEADS
    q_inner = d
    kv_inner = ATTN_KV_HEADS * ATTN_HEAD_DIM

    o_g = 4 * inner
    o_qa = o_g + n_gate
    o_ka = o_qa + q_inner
    o_va = o_ka + kv_inner
    o_br = o_va + kv_inner
    perm = _rope_head_perm()
    head_perm = lambda w: w.reshape(d, -1, ATTN_HEAD_DIM)[:, :, perm].reshape(d, -1)
    w_main = jnp.concatenate([w_in[:, 0:o_g], w_in[:, o_br:o_br + 2 * d], head_perm(w_in[:, o_qa:o_va]),
                              w_in[:, o_va:o_br]], axis=1).astype(BF16)
    conv_cols, mid_cols = 2 * inner, 2 * inner + 2 * d
    hm = MLSTM_HEADS
    wg = w_in[:, o_g:o_qa]
    gate_cols = lambda a: (jnp.concatenate([a[..., 0:hm], a[..., 2 * hm:3 * hm]], -1),
                           jnp.concatenate([a[..., hm:2 * hm], a[..., 3 * hm:4 * hm]], -1))
    wi, wf = gate_cols(wg)
    bi, bf = gate_cols(b_gates)
    lane_pad = lambda a: jnp.pad(a, [(0, 0)] * (a.ndim - 1) + [(0, LANES - a.shape[-1])])
    w_gates = jnp.concatenate([lane_pad(wi), lane_pad(wf)], -1).astype(BF16)
    bias_gates = jnp.concatenate([lane_pad(bi), lane_pad(bf)], -1).reshape(1, 2 * LANES)

    mod3 = mod.reshape(bsz, 1, mod.shape[-1])
    cos, sin = _rope_tables(s)
    dh_m = inner // MLSTM_HEADS
    qk, mid, q_rot, k_rot, v_t, gates = _in_proj(
        x, mod3, norm1_pre.reshape(1, d), w_main, w_gates, bias_gates, conv_w, conv_b, cos, sin,
        attn_qnorm[perm], attn_knorm[perm], conv_cols=conv_cols, mid_cols=mid_cols, qa_cols=q_inner,
        ka_cols=kv_inner, tm=min(s, 512), q_scale_m=dh_m ** -0.5,
        q_scale_a=ATTN_HEAD_DIM ** -0.5 * math.log2(math.e))

    h_f, h_b = _mlstm(qk, mid, gates, v_block=0, ln=min(s, 256))
    y_a = _attention(q_rot, k_rot, v_t, tq=min(s, 256))

    x1 = _merge(h_f, h_b, mid, y_a, x, mod3, mlstm_gn.reshape(1, d), norm1_post.reshape(1, d),
                w_branch_m.astype(BF16), w_branch_a.astype(BF16), w_out.astype(BF16),
                o_block=1, bm_block=2, ba_block=3, tm=min(s, 512))
    return _mlp(x1, mod3, norm2_pre.reshape(1, d), norm2_post.reshape(1, d),
                w_mlp_in.astype(BF16), w_mlp_out.astype(BF16), tm=min(s, 512), ff_chunk=1024)


def kernel(x, c, w_ada, b_ada, norm1_pre, norm1_post, w_in, b_gates, conv_w, conv_b, mlstm_gn, attn_qnorm, attn_knorm, w_branch_m, w_branch_a, w_out, norm2_pre, norm2_post, w_mlp_in, w_mlp_out):
    for l in range(w_ada.shape[0]):
        mod = _ada(c, w_ada[l], b_ada[l])
        x = _layer(x, mod, norm1_pre[l], norm1_post[l], w_in[l], b_gates[l], conv_w[l], conv_b[l],
                   mlstm_gn[l], attn_qnorm[l], attn_knorm[l], w_branch_m[l], w_branch_a[l], w_out[l],
                   norm2_pre[l], norm2_post[l], w_mlp_in[l], w_mlp_out[l])
    return x
```

```python
import functools
import math

import jax
import jax.numpy as jnp
from jax import lax
from jax.experimental import pallas as pl
from jax.experimental.pallas import tpu as pltpu

EPS = 1e-6
GRID_W = 64
ROPE_THETA = 10000.0
MLSTM_HEADS = 4
MLSTM_CONV_W = 5
ATTN_HEAD_DIM = 128
ATTN_KV_HEADS = 2
LANES = 128
HALO = 16
ONES_ROWS = 16

F32 = jnp.float32
BF16 = jnp.bfloat16
VMEM_LIMIT = 56 * 1024 * 1024


def _params(*sem):
    return pltpu.CompilerParams(dimension_semantics=sem, vmem_limit_bytes=VMEM_LIMIT)


def _sigmoid(x):
    return 1.0 / (1.0 + jnp.exp(-x))


def _dot(a, b):
    return jnp.dot(a, b, preferred_element_type=F32)


def _dot_nt(a, b):
    return lax.dot_general(a, b, (((1,), (1,)), ((), ())), preferred_element_type=F32)


def _dot_tn(a, b):
    return lax.dot_general(a, b, (((0,), (0,)), ((), ())), preferred_element_type=F32)


def _ada_kernel(c_ref, w_ref, b_ref, o_ref):
    c = c_ref[...]
    sc = c * _sigmoid(c)
    o_ref[...] = jnp.dot(sc, w_ref[...], preferred_element_type=F32,
                         precision=lax.Precision.HIGHEST) + b_ref[...]


def _ada(c, w, b):
    bsz, d = c.shape
    n = w.shape[1]
    tn = 1024
    return pl.pallas_call(
        _ada_kernel,
        grid=(n // tn,),
        in_specs=[pl.BlockSpec((bsz, d), lambda j: (0, 0)),
                  pl.BlockSpec((d, tn), lambda j: (0, j)),
                  pl.BlockSpec((1, tn), lambda j: (0, j))],
        out_specs=pl.BlockSpec((bsz, tn), lambda j: (0, j)),
        out_shape=jax.ShapeDtypeStruct((bsz, n), F32),
        compiler_params=_params("parallel"),
        name="ada",
    )(c, w, b.reshape(1, n))


def _rope_head_perm():
    quarter = ATTN_HEAD_DIM // 4
    blocks = [0, 2, 1, 3]
    return jnp.concatenate([jnp.arange(b * quarter, (b + 1) * quarter) for b in blocks])


def _norm_rope(x, w, cos, sin):
    ms = jnp.mean(x * x, axis=-1, keepdims=True)
    y = x * lax.rsqrt(ms + EPS) * w
    return y * cos + pltpu.roll(y, LANES // 2, 1) * sin


def _in_proj_kernel(xp_ref, x_ref, xn_ref, mod_ref, nw_ref, w_ref, wg_ref, bg_ref, cw_ref, cb_ref,
                    cos_ref, sin_ref, qw_ref, kw_ref,
                    qk_ref, mid_ref, qo_ref, ko_ref, vt_ref, g_ref, h_scr, e_scr,
                    *, chunk, q_scale_m, q_scale_a):
    t = pl.program_id(1)
    nt = pl.num_programs(1)
    tm, d = x_ref.shape
    conv_cols = qk_ref.shape[1]
    mid_cols = mid_ref.shape[1]
    qa_cols = qo_ref.shape[1]
    ka_cols = ko_ref.shape[1]
    gain = nw_ref[...] * (1.0 + mod_ref[:, d:2 * d])
    shift = mod_ref[:, 0:d]

    def normed(xv):
        ms = jnp.mean(xv * xv, axis=-1, keepdims=True)
        return (xv * lax.rsqrt(ms + EPS) * gain + shift).astype(BF16)

    h_scr[0:HALO, :] = normed(xp_ref[...])
    h_scr[HALO:HALO + tm, :] = normed(x_ref[...])
    h_scr[HALO + tm:, :] = normed(xn_ref[...])
    centre = slice(HALO, HALO + tm)
    g_ref[...] = _dot(h_scr[centre, :], wg_ref[...]) + bg_ref[...]

    half = HALO // 2
    pad = MLSTM_CONV_W // 2
    has_prev = (t > 0).astype(F32)
    has_next = (t < nt - 1).astype(F32)

    def conv_chunk(c0):
        cs = slice(c0, c0 + chunk)
        e = _dot(h_scr[...], w_ref[:, cs])
        e_scr[0:half, :] = e[half:HALO] * has_prev
        e_scr[half:half + tm, :] = e[HALO:HALO + tm]
        e_scr[half + tm:, :] = e[HALO + tm:HALO + tm + half] * has_next
        acc = None
        for j in range(MLSTM_CONV_W):
            term = e_scr[half - pad + j:half - pad + j + tm, :] * cw_ref[j:j + 1, cs]
            acc = term if acc is None else acc + term
        y = acc + cb_ref[:, cs]
        y = y * _sigmoid(y)
        if c0 < conv_cols // 2:
            y = y * q_scale_m
        qk_ref[:, cs] = y.astype(qk_ref.dtype)

    mid_base = conv_cols

    def mid_chunk(c0):
        mid_ref[:, c0:c0 + chunk] = _dot(h_scr[centre, :],
                                         w_ref[:, mid_base + c0:mid_base + c0 + chunk]).astype(mid_ref.dtype)

    qa_base = mid_base + mid_cols
    cos = cos_ref[...]
    sin = sin_ref[...]

    def q_chunk(c0):
        a = _dot(h_scr[centre, :], w_ref[:, qa_base + c0:qa_base + c0 + chunk])
        for hh in range(chunk // LANES):
            y = _norm_rope(a[:, hh * LANES:(hh + 1) * LANES], qw_ref[...], cos, sin)
            qo_ref[:, c0 + hh * LANES:c0 + (hh + 1) * LANES] = (y * q_scale_a).astype(qo_ref.dtype)

    kv_base = qa_base + qa_cols

    def kv_chunk():
        a = _dot(h_scr[centre, :], w_ref[:, kv_base:kv_base + 2 * ka_cols])
        for g in range(ka_cols // LANES):
            y = _norm_rope(a[:, g * LANES:(g + 1) * LANES], kw_ref[...], cos, sin)
            ko_ref[:, g * LANES:(g + 1) * LANES] = y.astype(ko_ref.dtype)
            r0 = g * (LANES + ONES_ROWS)
            v = a[:, ka_cols + g * LANES:ka_cols + (g + 1) * LANES].astype(vt_ref.dtype)
            vt_ref[r0:r0 + LANES, :] = v.T
            vt_ref[r0 + LANES:r0 + LANES + ONES_ROWS, :] = jnp.ones((ONES_ROWS, tm), vt_ref.dtype)

    heavy = ([functools.partial(conv_chunk, c0) for c0 in range(0, conv_cols, chunk)]
             + [functools.partial(q_chunk, c0) for c0 in range(0, qa_cols, chunk)] + [kv_chunk])
    light = [functools.partial(mid_chunk, c0) for c0 in range(0, mid_cols, chunk)]
    for i in range(max(len(heavy), len(light))):
        for task in heavy[i:i + 1] + light[i:i + 1]:
            task()


def _in_proj(x, mod3, norm_w, w_main, w_gates, b_gates, conv_w, conv_b, cos, sin, qn_w, kn_w,
             conv_cols, mid_cols, qa_cols, ka_cols, tm, q_scale_m, q_scale_a):
    bsz, s, d = x.shape
    n = w_main.shape[1]
    assert n == conv_cols + mid_cols + qa_cols + 2 * ka_cols
    ng = w_gates.shape[1]
    vt_rows = (ka_cols // LANES) * (LANES + ONES_ROWS)
    nh = s // HALO
    chunk = 512
    kern = functools.partial(_in_proj_kernel, chunk=chunk, q_scale_m=q_scale_m, q_scale_a=q_scale_a)
    tok = lambda b, t: (b, t, 0)
    const = lambda b, t: (0, 0)
    bf = lambda cols: jax.ShapeDtypeStruct((bsz, s, cols), BF16)
    return pl.pallas_call(
        kern,
        grid=(bsz, s // tm),
        in_specs=[pl.BlockSpec((None, HALO, d), lambda b, t: (b, jnp.maximum(t * (tm // HALO) - 1, 0), 0)),
                  pl.BlockSpec((None, tm, d), tok),
                  pl.BlockSpec((None, HALO, d),
                               lambda b, t: (b, jnp.minimum((t + 1) * (tm // HALO), nh - 1), 0)),
                  pl.BlockSpec((None, 1, mod3.shape[-1]), lambda b, t: (b, 0, 0)),
                  pl.BlockSpec((1, d), const),
                  pl.BlockSpec((d, n), const, pipeline_mode=pl.Buffered(1)),
                  pl.BlockSpec((d, ng), const),
                  pl.BlockSpec((1, ng), const),
                  pl.BlockSpec((MLSTM_CONV_W, conv_cols), const),
                  pl.BlockSpec((1, conv_cols), const),
                  pl.BlockSpec((tm, LANES), lambda b, t: (t, 0)),
                  pl.BlockSpec((tm, LANES), lambda b, t: (t, 0)),
                  pl.BlockSpec((1, LANES), const),
                  pl.BlockSpec((1, LANES), const)],
        out_specs=[pl.BlockSpec((None, tm, conv_cols), tok),
                   pl.BlockSpec((None, tm, mid_cols), tok),
                   pl.BlockSpec((None, tm, qa_cols), tok),
                   pl.BlockSpec((None, tm, ka_cols), tok),
                   pl.BlockSpec((None, None, vt_rows, tm), lambda b, t: (b, t, 0, 0)),
                   pl.BlockSpec((None, tm, ng), tok)],
        out_shape=[bf(conv_cols), bf(mid_cols), bf(qa_cols), bf(ka_cols),
                   jax.ShapeDtypeStruct((bsz, s // tm, vt_rows, tm), BF16),
                   jax.ShapeDtypeStruct((bsz, s, ng), F32)],
        scratch_shapes=[pltpu.VMEM((tm + 2 * HALO, d), BF16),
                        pltpu.VMEM((tm + HALO, chunk), F32)],
        compiler_params=_params("parallel", "parallel"),
        name="in_proj",
    )(x, x, x, mod3, norm_w, w_main, w_gates, b_gates, conv_w, conv_b.reshape(1, conv_cols), cos, sin,
      qn_w.reshape(1, LANES), kn_w.reshape(1, LANES))


def _mlstm_kernel(qkf_ref, qkb_ref, vf_ref, vb_ref, gf_ref, gb_ref, hf_ref, hb_ref,
                  c_scr, n_scr, m_scr):
    ln = vf_ref.shape[0]
    inner = vf_ref.shape[1]
    dh = inner // MLSTM_HEADS

    @pl.when(pl.program_id(1) == 0)
    def _():
        c_scr[...] = jnp.zeros_like(c_scr)
        n_scr[...] = jnp.zeros_like(n_scr)
        m_scr[...] = jnp.full_like(m_scr, -jnp.inf)

    row = lax.broadcasted_iota(jnp.int32, (ln, ln), 0)
    col = lax.broadcasted_iota(jnp.int32, (ln, ln), 1)
    lower = col <= row
    upper = col >= row

    def gate_terms(g_ref, tri):
        gi = g_ref[:, 0:LANES]
        gf = g_ref[:, LANES:2 * LANES]
        a = jnp.minimum(gf, 0.0) - jnp.log(1.0 + jnp.exp(-jnp.abs(gf)))
        b = jnp.dot(tri.astype(F32), a, preferred_element_type=F32, precision=lax.Precision.HIGHEST)
        u = gi - b
        return b, u, u.T

    b_f, u_f, ut_f = gate_terms(gf_ref, lower)
    b_b, u_b, ut_b = gate_terms(gb_ref, upper)

    for direction in range(2):
        qk_ref, v_ref, h_ref = ((qkf_ref, vf_ref, hf_ref), (qkb_ref, vb_ref, hb_ref))[direction]
        bmat, umat, utmat = ((b_f, u_f, ut_f), (b_b, u_b, ut_b))[direction]
        mask = (lower, upper)[direction]
        for head in range(MLSTM_HEADS):
            j = direction * MLSTM_HEADS + head
            q = qk_ref[:, head * dh:(head + 1) * dh]
            k = qk_ref[:, inner + head * dh:inner + (head + 1) * dh]
            v = v_ref[:, head * dh:(head + 1) * dh]
            b_col = bmat[:, j:j + 1]
            u_col = umat[:, j:j + 1]
            u_row = utmat[j:j + 1, :]
            m_prev = m_scr[j:j + 1, 0:1]
            c_old = c_scr[j]
            n_old = n_scr[j:j + 1, :]

            u_mat = jnp.where(mask, u_row, -jnp.inf)
            g_col = jnp.maximum(jnp.max(u_mat, axis=1, keepdims=True), m_prev)
            decay_mat = jnp.exp(u_mat - g_col)
            w = _dot_nt(q, k) * decay_mat
            inter = jnp.exp(m_prev - g_col)
            num = _dot(w.astype(BF16), v) + inter * _dot(q, c_old.astype(BF16))
            den = (jnp.sum(w, axis=1, keepdims=True)
                   + inter * jnp.sum(q.astype(F32) * n_old, axis=1, keepdims=True))
            floor = jnp.exp(-(g_col + b_col))
            h_ref[:, head * dh:(head + 1) * dh] = num / jnp.maximum(jnp.abs(den), floor)

            g_end = jnp.max(g_col, axis=0, keepdims=True)
            b_tot = jnp.min(b_col, axis=0, keepdims=True)
            carry = jnp.exp(m_prev - g_end)
            kw = k.astype(F32) * jnp.exp(u_col - g_end)
            c_scr[j] = carry * c_old + _dot_tn(kw.astype(BF16), v)
            n_scr[j:j + 1, :] = carry * n_old + jnp.sum(kw, axis=0, keepdims=True)
            m_scr[j:j + 1, :] = jnp.broadcast_to(b_tot + g_end, (1, m_scr.shape[1]))


def _mlstm(qk, mid, gates, v_block, ln):
    bsz, s, two_inner = qk.shape
    inner = two_inner // 2
    nc = s // ln
    ng = gates.shape[-1]
    dh = inner // MLSTM_HEADS
    nchain = 2 * MLSTM_HEADS
    fwd = lambda b, i: (b, i, 0)
    bwd = lambda b, i: (b, nc - 1 - i, 0)
    return pl.pallas_call(
        _mlstm_kernel,
        grid=(bsz, nc),
        in_specs=[pl.BlockSpec((None, ln, two_inner), fwd),
                  pl.BlockSpec((None, ln, two_inner), bwd),
                  pl.BlockSpec((None, ln, inner), lambda b, i: (b, i, v_block)),
                  pl.BlockSpec((None, ln, inner), lambda b, i: (b, nc - 1 - i, v_block)),
                  pl.BlockSpec((None, ln, ng), fwd),
                  pl.BlockSpec((None, ln, ng), bwd)],
        out_specs=[pl.BlockSpec((None, ln, inner), fwd),
                   pl.BlockSpec((None, ln, inner), bwd)],
        out_shape=[jax.ShapeDtypeStruct((bsz, s, inner), F32),
                   jax.ShapeDtypeStruct((bsz, s, inner), F32)],
        scratch_shapes=[pltpu.VMEM((nchain, dh, dh), F32),
                        pltpu.VMEM((nchain, dh), F32),
                        pltpu.VMEM((nchain, LANES), F32)],
        compiler_params=_params("parallel", "arbitrary"),
        name="mlstm",
    )(qk, qk, mid, mid, gates, gates)


def _rope_tab_kernel(freq_ref, cos_ref, sin_ref):
    ts = cos_ref.shape[0]
    t = pl.program_id(0) * ts + lax.broadcasted_iota(jnp.int32, (ts, LANES), 0)
    lane = lax.broadcasted_iota(jnp.int32, (ts, LANES), 1)
    grid_shift = GRID_W.bit_length() - 1
    is_col = (lane & (LANES // 4)) != 0
    pos = jnp.where(is_col, t & (GRID_W - 1), t >> grid_shift).astype(F32)
    ang = pos * freq_ref[...]
    sign = jnp.where(lane < LANES // 2, -1.0, 1.0)
    cos_ref[...] = jnp.cos(ang)
    sin_ref[...] = jnp.sin(ang) * sign


def _rope_tables(s):
    n_freq = ATTN_HEAD_DIM // 4
    freqs = ROPE_THETA ** (-jnp.arange(n_freq, dtype=F32) / n_freq)
    freq_lanes = jnp.tile(freqs, 4).reshape(1, ATTN_HEAD_DIM)
    ts = min(s, 1024)
    return pl.pallas_call(
        _rope_tab_kernel,
        grid=(s // ts,),
        in_specs=[pl.BlockSpec((1, LANES), lambda t: (0, 0))],
        out_specs=[pl.BlockSpec((ts, LANES), lambda t: (t, 0)),
                   pl.BlockSpec((ts, LANES), lambda t: (t, 0))],
        out_shape=[jax.ShapeDtypeStruct((s, LANES), F32),
                   jax.ShapeDtypeStruct((s, LANES), F32)],
        compiler_params=_params("parallel"),
        name="rope_tab",
    )(freq_lanes)


def _attn_kernel(q_ref, k_ref, vt_ref, o_ref, qs_scr, m_scr, alpha_scr, acc_scr, st_scr, p_scr):
    tq = q_ref.shape[0]
    group = q_ref.shape[1] // LANES
    nk, tk = k_ref.shape[0], k_ref.shape[1]
    dh = LANES
    strip = 128

    for h in range(group):
        qs_scr[:, h * tq:(h + 1) * tq] = q_ref[:, h * LANES:(h + 1) * LANES].T
    m_scr[...] = jnp.full_like(m_scr, -jnp.inf)
    acc_scr[...] = jnp.zeros_like(acc_scr)

    def scores(j, slot):
        st_scr[slot] = _dot(k_ref[j], qs_scr[...]).astype(BF16)

    def softmax(slot):
        for h in range(group):
            cols = slice(h * tq, (h + 1) * tq)
            m_old = m_scr[:, cols]
            cmax = jnp.max(st_scr[slot, :, cols], axis=0, keepdims=True).astype(F32)
            m_new = jnp.maximum(m_old, cmax)
            m_b = m_new.astype(BF16)
            for r in range(0, tk, strip):
                p_scr[slot, r:r + strip, cols] = jnp.exp2(st_scr[slot, r:r + strip, cols] - m_b)
            alpha_scr[slot, :, cols] = jnp.exp2(m_old - m_new)
            m_scr[:, cols] = m_new

    def pv(j, slot):
        acc_scr[...] = alpha_scr[slot] * acc_scr[...] + _dot(vt_ref[j], p_scr[slot])

    assert nk % 2 == 0
    scores(0, 0)
    scores(1, 1)
    softmax(0)

    def body(i, carry):
        j = 2 * i + 1
        scores(j + 1, 0)
        softmax(1)
        pv(j - 1, 0)
        scores(j + 2, 1)
        softmax(0)
        pv(j, 1)
        return carry

    lax.fori_loop(0, (nk - 2) // 2, body, 0)
    softmax(1)
    pv(nk - 2, 0)
    pv(nk - 1, 1)
    for h in range(group):
        cols = slice(h * tq, (h + 1) * tq)
        out_t = acc_scr[0:dh, cols] / acc_scr[dh:dh + 1, cols]
        o_ref[:, h * LANES:(h + 1) * LANES] = out_t.T.astype(o_ref.dtype)


def _attention(q, k, vt, tq):
    bsz, s, q_inner = q.shape
    dh = ATTN_HEAD_DIM
    nk, tk = vt.shape[1], vt.shape[3]
    kvh = k.shape[-1] // dh
    gw = q_inner // kvh
    group = gw // dh
    vrows = dh + ONES_ROWS
    k4 = k.reshape(bsz, nk, tk, kvh * dh)
    return pl.pallas_call(
        _attn_kernel,
        grid=(bsz, kvh, s // tq),
        in_specs=[pl.BlockSpec((None, tq, gw), lambda b, g, i: (b, i, g)),
                  pl.BlockSpec((None, nk, tk, dh), lambda b, g, i: (b, 0, 0, g)),
                  pl.BlockSpec((None, nk, vrows, tk), lambda b, g, i: (b, 0, g, 0))],
        out_specs=pl.BlockSpec((None, tq, gw), lambda b, g, i: (b, i, g)),
        out_shape=jax.ShapeDtypeStruct((bsz, s, q_inner), BF16),
        scratch_shapes=[pltpu.VMEM((dh, group * tq), BF16),
                        pltpu.VMEM((1, group * tq), F32),
                        pltpu.VMEM((2, 1, group * tq), F32),
                        pltpu.VMEM((vrows, group * tq), F32),
                        pltpu.VMEM((2, tk, group * tq), BF16),
                        pltpu.VMEM((2, tk, group * tq), BF16)],
        compiler_params=_params("parallel", "parallel", "parallel"),
        name="attn",
    )(q, k4, vt)


def _merge_kernel(hf_ref, hb_ref, o_ref, bm_ref, ba_ref, ya_ref, x_ref, mod_ref, gn_ref, nw_ref,
                  wm_ref, wa_ref, wo_ref, out_ref, ym_scr):
    d = x_ref.shape[-1]
    dh = d // MLSTM_HEADS
    for head in range(MLSTM_HEADS):
        sl = slice(head * dh, (head + 1) * dh)
        h = hf_ref[:, sl] + hb_ref[:, sl]
        mu = jnp.mean(h, axis=-1, keepdims=True)
        hc = h - mu
        var = jnp.mean(hc * hc, axis=-1, keepdims=True)
        hn = hc * lax.rsqrt(var + EPS) * gn_ref[:, sl]
        ym_scr[:, sl] = (hn * _sigmoid(o_ref[:, sl].astype(F32))).astype(BF16)
    pm = _dot(ym_scr[...], wm_ref[...])
    pa = _dot(ya_ref[...], wa_ref[...])
    y = _sigmoid(bm_ref[...].astype(F32)) * pm + _sigmoid(ba_ref[...].astype(F32)) * pa
    y2 = _dot(y.astype(BF16), wo_ref[...])
    ms = jnp.mean(y2 * y2, axis=-1, keepdims=True)
    gate = mod_ref[:, 2 * d:3 * d]
    out_ref[...] = x_ref[...] + gate * (y2 * lax.rsqrt(ms + EPS) * nw_ref[...])


def _merge(hf, hb, mid, ya, x, mod3, gn_w, norm_w, wm, wa, wo, o_block, bm_block, ba_block, tm):
    bsz, s, d = x.shape
    tok = lambda b, t: (b, t, 0)
    const = lambda b, t: (0, 0)
    return pl.pallas_call(
        _merge_kernel,
        grid=(bsz, s // tm),
        in_specs=[pl.BlockSpec((None, tm, d), tok),
                  pl.BlockSpec((None, tm, d), tok),
                  pl.BlockSpec((None, tm, d), lambda b, t: (b, t, o_block)),
                  pl.BlockSpec((None, tm, d), lambda b, t: (b, t, bm_block)),
                  pl.BlockSpec((None, tm, d), lambda b, t: (b, t, ba_block)),
                  pl.BlockSpec((None, tm, d), tok),
                  pl.BlockSpec((None, tm, d), tok),
                  pl.BlockSpec((None, 1, mod3.shape[-1]), lambda b, t: (b, 0, 0)),
                  pl.BlockSpec((1, d), const),
                  pl.BlockSpec((1, d), const),
                  pl.BlockSpec((d, d), const),
                  pl.BlockSpec((d, d), const),
                  pl.BlockSpec((d, d), const)],
        out_specs=pl.BlockSpec((None, tm, d), tok),
        out_shape=jax.ShapeDtypeStruct((bsz, s, d), F32),
        scratch_shapes=[pltpu.VMEM((tm, d), BF16)],
        compiler_params=_params("parallel", "parallel"),
        name="merge",
    )(hf, hb, mid, mid, mid, ya, x, mod3, gn_w, norm_w, wm, wa, wo)


def _mlp_kernel(x_ref, mod_ref, n1_ref, n2_ref, w1_ref, w2_ref, out_ref, *, ff_chunk):
    d = x_ref.shape[-1]
    x = x_ref[...]
    ms = jnp.mean(x * x, axis=-1, keepdims=True)
    shift = mod_ref[:, 3 * d:4 * d]
    scale = mod_ref[:, 4 * d:5 * d]
    gate = mod_ref[:, 5 * d:6 * d]
    h = (x * lax.rsqrt(ms + EPS) * n1_ref[...] * (1.0 + scale) + shift).astype(BF16)
    acc = None
    for c in range(w1_ref.shape[1] // ff_chunk):
        sl = slice(c * ff_chunk, (c + 1) * ff_chunk)
        u = jnp.maximum(_dot(h, w1_ref[:, sl]), 0.0)
        part = _dot((u * u).astype(BF16), w2_ref[sl, :])
        acc = part if acc is None else acc + part
    ms2 = jnp.mean(acc * acc, axis=-1, keepdims=True)
    out_ref[...] = x + gate * (acc * lax.rsqrt(ms2 + EPS) * n2_ref[...])


def _mlp(x, mod3, n1, n2, w1, w2, tm, ff_chunk):
    bsz, s, d = x.shape
    ff = w1.shape[1]
    tok = lambda b, t: (b, t, 0)
    const = lambda b, t: (0, 0)
    kern = functools.partial(_mlp_kernel, ff_chunk=ff_chunk)
    return pl.pallas_call(
        kern,
        grid=(bsz, s // tm),
        in_specs=[pl.BlockSpec((None, tm, d), tok),
                  pl.BlockSpec((None, 1, mod3.shape[-1]), lambda b, t: (b, 0, 0)),
                  pl.BlockSpec((1, d), const),
                  pl.BlockSpec((1, d), const),
                  pl.BlockSpec((d, ff), const, pipeline_mode=pl.Buffered(1)),
                  pl.BlockSpec((ff, d), const, pipeline_mode=pl.Buffered(1))],
        out_specs=pl.BlockSpec((None, tm, d), tok),
        out_shape=jax.ShapeDtypeStruct((bsz, s, d), F32),
        compiler_params=_params("parallel", "parallel"),
        name="mlp",
    )(x, mod3, n1, n2, w1, w2)


def _layer(x, mod, norm1_pre, norm1_post, w_in, b_gates, conv_w, conv_b, mlstm_gn, attn_qnorm,
           attn_knorm, w_branch_m, w_branch_a, w_out, norm2_pre, norm2_post, w_mlp_in, w_mlp_out):
    bsz, s, d = x.shape
    inner = d
    n_gate = 4 * MLSTM_HEADS
    q_inner = d
    kv_inner = ATTN_KV_HEADS * ATTN_HEAD_DIM

    o_g = 4 * inner
    o_qa = o_g + n_gate
    o_ka = o_qa + q_inner
    o_va = o_ka + kv_inner
    o_br = o_va + kv_inner
    perm = _rope_head_perm()
    head_perm = lambda w: w.reshape(d, -1, ATTN_HEAD_DIM)[:, :, perm].reshape(d, -1)
    w_main = jnp.concatenate([w_in[:, 0:o_g], w_in[:, o_br:o_br + 2 * d], head_perm(w_in[:, o_qa:o_va]),
                              w_in[:, o_va:o_br]], axis=1).astype(BF16)
    conv_cols, mid_cols = 2 * inner, 2 * inner + 2 * d
    hm = MLSTM_HEADS
    wg = w_in[:, o_g:o_qa]
    gate_cols = lambda a: (jnp.concatenate([a[..., 0:hm], a[..., 2 * hm:3 * hm]], -1),
                           jnp.concatenate([a[..., hm:2 * hm], a[..., 3 * hm:4 * hm]], -1))
    wi, wf = gate_cols(wg)
    bi, bf = gate_cols(b_gates)
    lane_pad = lambda a: jnp.pad(a, [(0, 0)] * (a.ndim - 1) + [(0, LANES - a.shape[-1])])
    w_gates = jnp.concatenate([lane_pad(wi), lane_pad(wf)], -1).astype(BF16)
    bias_gates = jnp.concatenate([lane_pad(bi), lane_pad(bf)], -1).reshape(1, 2 * LANES)

    mod3 = mod.reshape(bsz, 1, mod.shape[-1])
    cos, sin = _rope_tables(s)
    dh_m = inner // MLSTM_HEADS
    qk, mid, q_rot, k_rot, v_t, gates = _in_proj(
        x, mod3, norm1_pre.reshape(1, d), w_main, w_gates, bias_gates, conv_w, conv_b, cos, sin,
        attn_qnorm[perm], attn_knorm[perm], conv_cols=conv_cols, mid_cols=mid_cols, qa_cols=q_inner,
        ka_cols=kv_inner, tm=min(s, 512), q_scale_m=dh_m ** -0.5,
        q_scale_a=ATTN_HEAD_DIM ** -0.5 * math.log2(math.e))

    h_f, h_b = _mlstm(qk, mid, gates, v_block=0, ln=min(s, 256))
    y_a = _attention(q_rot, k_rot, v_t, tq=min(s, 1024))

    x1 = _merge(h_f, h_b, mid, y_a, x, mod3, mlstm_gn.reshape(1, d), norm1_post.reshape(1, d),
                w_branch_m.astype(BF16), w_branch_a.astype(BF16), w_out.astype(BF16),
                o_block=1, bm_block=2, ba_block=3, tm=min(s, 512))
    return _mlp(x1, mod3, norm2_pre.reshape(1, d), norm2_post.reshape(1, d),
                w_mlp_in.astype(BF16), w_mlp_out.astype(BF16), tm=min(s, 512), ff_chunk=1024)


def kernel(x, c, w_ada, b_ada, norm1_pre, norm1_post, w_in, b_gates, conv_w, conv_b, mlstm_gn, attn_qnorm, attn_knorm, w_branch_m, w_branch_a, w_out, norm2_pre, norm2_post, w_mlp_in, w_mlp_out):
    for l in range(w_ada.shape[0]):
        mod = _ada(c, w_ada[l], b_ada[l])
        x = _layer(x, mod, norm1_pre[l], norm1_post[l], w_in[l], b_gates[l], conv_w[l], conv_b[l],
                   mlstm_gn[l], attn_qnorm[l], attn_knorm[l], w_branch_m[l], w_branch_a[l], w_out[l],
                   norm2_pre[l], norm2_post[l], w_mlp_in[l], w_mlp_out[l])
    return x
```

```python
import functools
import math

import jax
import jax.numpy as jnp
from jax import lax
from jax.experimental import pallas as pl
from jax.experimental.pallas import tpu as pltpu

EPS = 1e-6
GRID_W = 64
ROPE_THETA = 10000.0
MLSTM_HEADS = 4
MLSTM_CONV_W = 5
ATTN_HEAD_DIM = 128
ATTN_KV_HEADS = 2
LANES = 128
HALO = 16
ONES_ROWS = 16

F32 = jnp.float32
BF16 = jnp.bfloat16
VMEM_LIMIT = 56 * 1024 * 1024


def _params(*sem):
    return pltpu.CompilerParams(dimension_semantics=sem, vmem_limit_bytes=VMEM_LIMIT)


def _sigmoid(x):
    return 1.0 / (1.0 + jnp.exp(-x))


def _dot(a, b):
    return jnp.dot(a, b, preferred_element_type=F32)


def _dot_nt(a, b):
    return lax.dot_general(a, b, (((1,), (1,)), ((), ())), preferred_element_type=F32)


def _ada_kernel(c_ref, w_ref, b_ref, o_ref):
    c = c_ref[...]
    sc = c * _sigmoid(c)
    o_ref[...] = jnp.dot(sc, w_ref[...], preferred_element_type=F32,
                         precision=lax.Precision.HIGHEST) + b_ref[...]


def _ada(c, w, b):
    bsz, d = c.shape
    n = w.shape[1]
    tn = 1024
    return pl.pallas_call(
        _ada_kernel,
        grid=(n // tn,),
        in_specs=[pl.BlockSpec((bsz, d), lambda j: (0, 0)),
                  pl.BlockSpec((d, tn), lambda j: (0, j)),
                  pl.BlockSpec((1, tn), lambda j: (0, j))],
        out_specs=pl.BlockSpec((bsz, tn), lambda j: (0, j)),
        out_shape=jax.ShapeDtypeStruct((bsz, n), F32),
        compiler_params=_params("parallel"),
        name="ada",
    )(c, w, b.reshape(1, n))


def _rope_head_perm():
    quarter = ATTN_HEAD_DIM // 4
    blocks = [0, 2, 1, 3]
    return jnp.concatenate([jnp.arange(b * quarter, (b + 1) * quarter) for b in blocks])


def _norm_rope(x, w, cos, sin):
    ms = jnp.mean(x * x, axis=-1, keepdims=True)
    y = x * lax.rsqrt(ms + EPS) * w
    return y * cos + pltpu.roll(y, LANES // 2, 1) * sin


def _in_proj_kernel(xp_ref, x_ref, xn_ref, mod_ref, nw_ref, w_ref, wg_ref, bg_ref, cw_ref, cb_ref,
                    cos_ref, sin_ref, qw_ref, kw_ref,
                    qk_ref, vm_ref, mid_ref, qo_ref, ko_ref, vt_ref, g_ref, h_scr, e_scr,
                    *, chunk, q_scale_m, q_scale_a):
    t = pl.program_id(1)
    nt = pl.num_programs(1)
    tm, d = x_ref.shape
    conv_cols = qk_ref.shape[1]
    dh_m = conv_cols // (2 * MLSTM_HEADS)
    mid_cols = mid_ref.shape[1]
    qa_cols = qo_ref.shape[1]
    ka_cols = ko_ref.shape[1]
    gain = nw_ref[...] * (1.0 + mod_ref[:, d:2 * d])
    shift = mod_ref[:, 0:d]

    def normed(xv):
        ms = jnp.mean(xv * xv, axis=-1, keepdims=True)
        return (xv * lax.rsqrt(ms + EPS) * gain + shift).astype(BF16)

    h_scr[0:HALO, :] = normed(xp_ref[...])
    h_scr[HALO:HALO + tm, :] = normed(x_ref[...])
    h_scr[HALO + tm:, :] = normed(xn_ref[...])
    centre = slice(HALO, HALO + tm)
    g_ref[...] = _dot(h_scr[centre, :], wg_ref[...]) + bg_ref[...]

    half = HALO // 2
    pad = MLSTM_CONV_W // 2
    has_prev = (t > 0).astype(F32)
    has_next = (t < nt - 1).astype(F32)

    def conv_chunk(c0):
        cs = slice(c0, c0 + chunk)
        e = _dot(h_scr[...], w_ref[:, cs])
        e_scr[0:half, :] = e[half:HALO] * has_prev
        e_scr[half:half + tm, :] = e[HALO:HALO + tm]
        e_scr[half + tm:, :] = e[HALO + tm:HALO + tm + half] * has_next
        acc = None
        for j in range(MLSTM_CONV_W):
            term = e_scr[half - pad + j:half - pad + j + tm, :] * cw_ref[j:j + 1, cs]
            acc = term if acc is None else acc + term
        y = acc + cb_ref[:, cs]
        y = y * _sigmoid(y)
        if c0 < conv_cols // 2:
            y = y * q_scale_m
        qk_ref[:, cs] = y.astype(qk_ref.dtype)

    vm_base = conv_cols
    vm_cols = (vm_ref.shape[0] // (dh_m + ONES_ROWS)) * dh_m

    def vm_chunk(c0):
        a = _dot(h_scr[centre, :], w_ref[:, vm_base + c0:vm_base + c0 + chunk]).astype(vm_ref.dtype)
        for hh in range(chunk // dh_m):
            r0 = (c0 // dh_m + hh) * (dh_m + ONES_ROWS)
            for c in range(0, dh_m, LANES):
                vm_ref[r0 + c:r0 + c + LANES, :] = a[:, hh * dh_m + c:hh * dh_m + c + LANES].T
            vm_ref[r0 + dh_m:r0 + dh_m + ONES_ROWS, :] = jnp.ones((ONES_ROWS, tm), vm_ref.dtype)

    mid_base = vm_base + vm_cols

    def mid_chunk(c0):
        mid_ref[:, c0:c0 + chunk] = _dot(h_scr[centre, :],
                                         w_ref[:, mid_base + c0:mid_base + c0 + chunk]).astype(mid_ref.dtype)

    qa_base = mid_base + mid_cols
    cos = cos_ref[...]
    sin = sin_ref[...]

    def q_chunk(c0):
        a = _dot(h_scr[centre, :], w_ref[:, qa_base + c0:qa_base + c0 + chunk])
        for hh in range(chunk // LANES):
            y = _norm_rope(a[:, hh * LANES:(hh + 1) * LANES], qw_ref[...], cos, sin)
            qo_ref[:, c0 + hh * LANES:c0 + (hh + 1) * LANES] = (y * q_scale_a).astype(qo_ref.dtype)

    kv_base = qa_base + qa_cols

    def kv_chunk():
        a = _dot(h_scr[centre, :], w_ref[:, kv_base:kv_base + 2 * ka_cols])
        for g in range(ka_cols // LANES):
            y = _norm_rope(a[:, g * LANES:(g + 1) * LANES], kw_ref[...], cos, sin)
            ko_ref[:, g * LANES:(g + 1) * LANES] = y.astype(ko_ref.dtype)
            r0 = g * (LANES + ONES_ROWS)
            v = a[:, ka_cols + g * LANES:ka_cols + (g + 1) * LANES].astype(vt_ref.dtype)
            vt_ref[r0:r0 + LANES, :] = v.T
            vt_ref[r0 + LANES:r0 + LANES + ONES_ROWS, :] = jnp.ones((ONES_ROWS, tm), vt_ref.dtype)

    heavy = ([functools.partial(conv_chunk, c0) for c0 in range(0, conv_cols, chunk)]
             + [functools.partial(q_chunk, c0) for c0 in range(0, qa_cols, chunk)] + [kv_chunk])
    light = ([functools.partial(vm_chunk, c0) for c0 in range(0, vm_cols, chunk)]
             + [functools.partial(mid_chunk, c0) for c0 in range(0, mid_cols, chunk)])
    for i in range(max(len(heavy), len(light))):
        for task in heavy[i:i + 1] + light[i:i + 1]:
            task()


def _in_proj(x, mod3, norm_w, w_main, w_gates, b_gates, conv_w, conv_b, cos, sin, qn_w, kn_w,
             conv_cols, mid_cols, qa_cols, ka_cols, tm, q_scale_m, q_scale_a):
    bsz, s, d = x.shape
    n = w_main.shape[1]
    vm_cols = conv_cols // 2
    assert n == conv_cols + vm_cols + mid_cols + qa_cols + 2 * ka_cols
    ng = w_gates.shape[1]
    vm_rows = MLSTM_HEADS * (vm_cols // MLSTM_HEADS + ONES_ROWS)
    vt_rows = (ka_cols // LANES) * (LANES + ONES_ROWS)
    nh = s // HALO
    chunk = 512
    kern = functools.partial(_in_proj_kernel, chunk=chunk, q_scale_m=q_scale_m, q_scale_a=q_scale_a)
    tok = lambda b, t: (b, t, 0)
    const = lambda b, t: (0, 0)
    bf = lambda cols: jax.ShapeDtypeStruct((bsz, s, cols), BF16)
    return pl.pallas_call(
        kern,
        grid=(bsz, s // tm),
        in_specs=[pl.BlockSpec((None, HALO, d), lambda b, t: (b, jnp.maximum(t * (tm // HALO) - 1, 0), 0)),
                  pl.BlockSpec((None, tm, d), tok),
                  pl.BlockSpec((None, HALO, d),
                               lambda b, t: (b, jnp.minimum((t + 1) * (tm // HALO), nh - 1), 0)),
                  pl.BlockSpec((None, 1, mod3.shape[-1]), lambda b, t: (b, 0, 0)),
                  pl.BlockSpec((1, d), const),
                  pl.BlockSpec((d, n), const, pipeline_mode=pl.Buffered(1)),
                  pl.BlockSpec((d, ng), const),
                  pl.BlockSpec((1, ng), const),
                  pl.BlockSpec((MLSTM_CONV_W, conv_cols), const),
                  pl.BlockSpec((1, conv_cols), const),
                  pl.BlockSpec((tm, LANES), lambda b, t: (t, 0)),
                  pl.BlockSpec((tm, LANES), lambda b, t: (t, 0)),
                  pl.BlockSpec((1, LANES), const),
                  pl.BlockSpec((1, LANES), const)],
        out_specs=[pl.BlockSpec((None, tm, conv_cols), tok),
                   pl.BlockSpec((None, None, vm_rows, tm), lambda b, t: (b, t, 0, 0)),
                   pl.BlockSpec((None, tm, mid_cols), tok),
                   pl.BlockSpec((None, tm, qa_cols), tok),
                   pl.BlockSpec((None, tm, ka_cols), tok),
                   pl.BlockSpec((None, None, vt_rows, tm), lambda b, t: (b, t, 0, 0)),
                   pl.BlockSpec((None, tm, ng), tok)],
        out_shape=[bf(conv_cols),
                   jax.ShapeDtypeStruct((bsz, s // tm, vm_rows, tm), BF16),
                   bf(mid_cols), bf(qa_cols), bf(ka_cols),
                   jax.ShapeDtypeStruct((bsz, s // tm, vt_rows, tm), BF16),
                   jax.ShapeDtypeStruct((bsz, s, ng), F32)],
        scratch_shapes=[pltpu.VMEM((tm + 2 * HALO, d), BF16),
                        pltpu.VMEM((tm + HALO, chunk), F32)],
        compiler_params=_params("parallel", "parallel"),
        name="in_proj",
    )(x, x, x, mod3, norm_w, w_main, w_gates, b_gates, conv_w, conv_b.reshape(1, conv_cols), cos, sin,
      qn_w.reshape(1, LANES), kn_w.reshape(1, LANES))


def _mlstm_kernel(qkf_ref, qkb_ref, vtf_ref, vtb_ref, gf_ref, gb_ref, hf_ref, hb_ref, st_scr, m_scr):
    ln = qkf_ref.shape[0]
    inner = qkf_ref.shape[1] // 2
    dh = inner // MLSTM_HEADS
    vrows = dh + ONES_ROWS

    @pl.when(pl.program_id(1) == 0)
    def _():
        st_scr[...] = jnp.zeros_like(st_scr)
        m_scr[...] = jnp.full_like(m_scr, -jnp.inf)

    row = lax.broadcasted_iota(jnp.int32, (ln, ln), 0)
    col = lax.broadcasted_iota(jnp.int32, (ln, ln), 1)
    lower = col <= row
    upper = col >= row

    def gate_terms(g_ref, tri):
        gi = g_ref[:, 0:LANES]
        gf = g_ref[:, LANES:2 * LANES]
        a = jnp.minimum(gf, 0.0) - jnp.log(1.0 + jnp.exp(-jnp.abs(gf)))
        tri_b = jnp.where(tri, 1.0, 0.0).astype(BF16)
        a1 = a.astype(BF16)
        r1 = a - a1.astype(F32)
        a2 = r1.astype(BF16)
        a3 = (r1 - a2.astype(F32)).astype(BF16)
        b = _dot(tri_b, a1) + _dot(tri_b, a2) + _dot(tri_b, a3)
        return gi - b, b.T

    u_f, bt_f = gate_terms(gf_ref, lower)
    u_b, bt_b = gate_terms(gb_ref, upper)

    for direction in range(2):
        qk_ref, vt_ref, h_ref = ((qkf_ref, vtf_ref, hf_ref), (qkb_ref, vtb_ref, hb_ref))[direction]
        umat, btmat = ((u_f, bt_f), (u_b, bt_b))[direction]
        mask = (upper, lower)[direction]
        for head in range(MLSTM_HEADS):
            j = direction * MLSTM_HEADS + head
            q = qk_ref[:, head * dh:(head + 1) * dh]
            k = qk_ref[:, inner + head * dh:inner + (head + 1) * dh]
            vt = vt_ref[head * vrows:(head + 1) * vrows, :]
            u_col = umat[:, j:j + 1]
            b_row = btmat[j:j + 1, :]
            m_prev = m_scr[j:j + 1, 0:1]
            state = st_scr[j]

            u_m = jnp.where(mask, jnp.broadcast_to(u_col, (ln, ln)), -jnp.inf)
            g_row = jnp.maximum(jnp.max(u_m, axis=0, keepdims=True), m_prev)
            w_t = _dot_nt(k, q) * jnp.exp(u_m - g_row)
            inter = jnp.exp(m_prev - g_row)
            tot = _dot(vt, w_t.astype(BF16)) + inter * _dot_nt(state.astype(BF16), q)
            floor = jnp.exp(-(g_row + b_row))
            h_t = tot[0:dh, :] / jnp.maximum(jnp.abs(tot[dh:dh + 1, :]), floor)
            h_ref[:, head * dh:(head + 1) * dh] = h_t.T

            g_end = jnp.max(g_row, axis=1, keepdims=True)
            b_tot = jnp.min(b_row, axis=1, keepdims=True)
            kw = k.astype(F32) * jnp.exp(jnp.broadcast_to(u_col, (ln, dh)) - g_end)
            st_scr[j] = jnp.exp(m_prev - g_end) * state + _dot(vt, kw.astype(BF16))
            m_scr[j:j + 1, :] = jnp.broadcast_to(b_tot + g_end, (1, m_scr.shape[1]))


def _mlstm(qk, vt, gates, ln):
    bsz, s, two_inner = qk.shape
    inner = two_inner // 2
    nc = s // ln
    ng = gates.shape[-1]
    dh = inner // MLSTM_HEADS
    nchain = 2 * MLSTM_HEADS
    vt_rows, tv = vt.shape[2], vt.shape[3]
    per = tv // ln
    fwd = lambda b, i: (b, i, 0)
    bwd = lambda b, i: (b, nc - 1 - i, 0)
    return pl.pallas_call(
        _mlstm_kernel,
        grid=(bsz, nc),
        in_specs=[pl.BlockSpec((None, ln, two_inner), fwd),
                  pl.BlockSpec((None, ln, two_inner), bwd),
                  pl.BlockSpec((None, None, vt_rows, ln), lambda b, i: (b, i // per, 0, i % per)),
                  pl.BlockSpec((None, None, vt_rows, ln),
                               lambda b, i: (b, (nc - 1 - i) // per, 0, (nc - 1 - i) % per)),
                  pl.BlockSpec((None, ln, ng), fwd),
                  pl.BlockSpec((None, ln, ng), bwd)],
        out_specs=[pl.BlockSpec((None, ln, inner), fwd),
                   pl.BlockSpec((None, ln, inner), bwd)],
        out_shape=[jax.ShapeDtypeStruct((bsz, s, inner), F32),
                   jax.ShapeDtypeStruct((bsz, s, inner), F32)],
        scratch_shapes=[pltpu.VMEM((nchain, dh + ONES_ROWS, dh), F32),
                        pltpu.VMEM((nchain, LANES), F32)],
        compiler_params=_params("parallel", "arbitrary"),
        name="mlstm",
    )(qk, qk, vt, vt, gates, gates)


def _rope_tab_kernel(freq_ref, cos_ref, sin_ref):
    ts = cos_ref.shape[0]
    t = pl.program_id(0) * ts + lax.broadcasted_iota(jnp.int32, (ts, LANES), 0)
    lane = lax.broadcasted_iota(jnp.int32, (ts, LANES), 1)
    grid_shift = GRID_W.bit_length() - 1
    is_col = (lane & (LANES // 4)) != 0
    pos = jnp.where(is_col, t & (GRID_W - 1), t >> grid_shift).astype(F32)
    ang = pos * freq_ref[...]
    sign = jnp.where(lane < LANES // 2, -1.0, 1.0)
    cos_ref[...] = jnp.cos(ang)
    sin_ref[...] = jnp.sin(ang) * sign


def _rope_tables(s):
    n_freq = ATTN_HEAD_DIM // 4
    freqs = ROPE_THETA ** (-jnp.arange(n_freq, dtype=F32) / n_freq)
    freq_lanes = jnp.tile(freqs, 4).reshape(1, ATTN_HEAD_DIM)
    ts = min(s, 1024)
    return pl.pallas_call(
        _rope_tab_kernel,
        grid=(s // ts,),
        in_specs=[pl.BlockSpec((1, LANES), lambda t: (0, 0))],
        out_specs=[pl.BlockSpec((ts, LANES), lambda t: (t, 0)),
                   pl.BlockSpec((ts, LANES), lambda t: (t, 0))],
        out_shape=[jax.ShapeDtypeStruct((s, LANES), F32),
                   jax.ShapeDtypeStruct((s, LANES), F32)],
        compiler_params=_params("parallel"),
        name="rope_tab",
    )(freq_lanes)


def _attn_kernel(q_ref, k_ref, vt_ref, o_ref, qs_scr, m_scr, alpha_scr, acc_scr, st_scr, p_scr):
    tq = q_ref.shape[0]
    group = q_ref.shape[1] // LANES
    nk, tk = k_ref.shape[0], k_ref.shape[1]
    dh = LANES
    strip = 128

    for h in range(group):
        qs_scr[:, h * tq:(h + 1) * tq] = q_ref[:, h * LANES:(h + 1) * LANES].T
    m_scr[...] = jnp.full_like(m_scr, -jnp.inf)
    acc_scr[...] = jnp.zeros_like(acc_scr)

    def scores(j, slot):
        st_scr[slot] = _dot(k_ref[j], qs_scr[...]).astype(BF16)

    def softmax(slot):
        for h in range(group):
            cols = slice(h * tq, (h + 1) * tq)
            m_old = m_scr[:, cols]
            cmax = jnp.max(st_scr[slot, :, cols], axis=0, keepdims=True).astype(F32)
            m_new = jnp.maximum(m_old, cmax)
            m_b = m_new.astype(BF16)
            for r in range(0, tk, strip):
                p_scr[slot, r:r + strip, cols] = jnp.exp2(st_scr[slot, r:r + strip, cols] - m_b)
            alpha_scr[slot, :, cols] = jnp.exp2(m_old - m_new)
            m_scr[:, cols] = m_new

    def pv(j, slot):
        acc_scr[...] = alpha_scr[slot] * acc_scr[...] + _dot(vt_ref[j], p_scr[slot])

    assert nk % 2 == 0
    scores(0, 0)
    scores(1, 1)
    softmax(0)

    def body(i, carry):
        j = 2 * i + 1
        scores(j + 1, 0)
        softmax(1)
        pv(j - 1, 0)
        scores(j + 2, 1)
        softmax(0)
        pv(j, 1)
        return carry

    lax.fori_loop(0, (nk - 2) // 2, body, 0)
    softmax(1)
    pv(nk - 2, 0)
    pv(nk - 1, 1)
    for h in range(group):
        cols = slice(h * tq, (h + 1) * tq)
        out_t = acc_scr[0:dh, cols] / acc_scr[dh:dh + 1, cols]
        o_ref[:, h * LANES:(h + 1) * LANES] = out_t.T.astype(o_ref.dtype)


def _attention(q, k, vt, tq):
    bsz, s, q_inner = q.shape
    dh = ATTN_HEAD_DIM
    nk, tk = vt.shape[1], vt.shape[3]
    kvh = k.shape[-1] // dh
    gw = q_inner // kvh
    group = gw // dh
    vrows = dh + ONES_ROWS
    k4 = k.reshape(bsz, nk, tk, kvh * dh)
    return pl.pallas_call(
        _attn_kernel,
        grid=(bsz, kvh, s // tq),
        in_specs=[pl.BlockSpec((None, tq, gw), lambda b, g, i: (b, i, g)),
                  pl.BlockSpec((None, nk, tk, dh), lambda b, g, i: (b, 0, 0, g)),
                  pl.BlockSpec((None, nk, vrows, tk), lambda b, g, i: (b, 0, g, 0))],
        out_specs=pl.BlockSpec((None, tq, gw), lambda b, g, i: (b, i, g)),
        out_shape=jax.ShapeDtypeStruct((bsz, s, q_inner), BF16),
        scratch_shapes=[pltpu.VMEM((dh, group * tq), BF16),
                        pltpu.VMEM((1, group * tq), F32),
                        pltpu.VMEM((2, 1, group * tq), F32),
                        pltpu.VMEM((vrows, group * tq), F32),
                        pltpu.VMEM((2, tk, group * tq), BF16),
                        pltpu.VMEM((2, tk, group * tq), BF16)],
        compiler_params=_params("parallel", "parallel", "parallel"),
        name="attn",
    )(q, k4, vt)


def _merge_kernel(hf_ref, hb_ref, o_ref, bm_ref, ba_ref, ya_ref, x_ref, mod_ref, gn_ref, nw_ref,
                  wm_ref, wa_ref, wo_ref, out_ref, ym_scr):
    d = x_ref.shape[-1]
    dh = d // MLSTM_HEADS
    for head in range(MLSTM_HEADS):
        sl = slice(head * dh, (head + 1) * dh)
        h = hf_ref[:, sl] + hb_ref[:, sl]
        mu = jnp.mean(h, axis=-1, keepdims=True)
        hc = h - mu
        var = jnp.mean(hc * hc, axis=-1, keepdims=True)
        hn = hc * lax.rsqrt(var + EPS) * gn_ref[:, sl]
        ym_scr[:, sl] = (hn * _sigmoid(o_ref[:, sl].astype(F32))).astype(BF16)
    pm = _dot(ym_scr[...], wm_ref[...])
    pa = _dot(ya_ref[...], wa_ref[...])
    y = _sigmoid(bm_ref[...].astype(F32)) * pm + _sigmoid(ba_ref[...].astype(F32)) * pa
    y2 = _dot(y.astype(BF16), wo_ref[...])
    ms = jnp.mean(y2 * y2, axis=-1, keepdims=True)
    gate = mod_ref[:, 2 * d:3 * d]
    out_ref[...] = x_ref[...] + gate * (y2 * lax.rsqrt(ms + EPS) * nw_ref[...])


def _merge(hf, hb, mid, ya, x, mod3, gn_w, norm_w, wm, wa, wo, o_block, bm_block, ba_block, tm):
    bsz, s, d = x.shape
    tok = lambda b, t: (b, t, 0)
    const = lambda b, t: (0, 0)
    return pl.pallas_call(
        _merge_kernel,
        grid=(bsz, s // tm),
        in_specs=[pl.BlockSpec((None, tm, d), tok),
                  pl.BlockSpec((None, tm, d), tok),
                  pl.BlockSpec((None, tm, d), lambda b, t: (b, t, o_block)),
                  pl.BlockSpec((None, tm, d), lambda b, t: (b, t, bm_block)),
                  pl.BlockSpec((None, tm, d), lambda b, t: (b, t, ba_block)),
                  pl.BlockSpec((None, tm, d), tok),
                  pl.BlockSpec((None, tm, d), tok),
                  pl.BlockSpec((None, 1, mod3.shape[-1]), lambda b, t: (b, 0, 0)),
                  pl.BlockSpec((1, d), const),
                  pl.BlockSpec((1, d), const),
                  pl.BlockSpec((d, d), const),
                  pl.BlockSpec((d, d), const),
                  pl.BlockSpec((d, d), const)],
        out_specs=pl.BlockSpec((None, tm, d), tok),
        out_shape=jax.ShapeDtypeStruct((bsz, s, d), F32),
        scratch_shapes=[pltpu.VMEM((tm, d), BF16)],
        compiler_params=_params("parallel", "parallel"),
        name="merge",
    )(hf, hb, mid, mid, mid, ya, x, mod3, gn_w, norm_w, wm, wa, wo)


def _mlp_kernel(x_ref, mod_ref, n1_ref, n2_ref, w1_ref, w2_ref, out_ref, *, ff_chunk):
    d = x_ref.shape[-1]
    x = x_ref[...]
    ms = jnp.mean(x * x, axis=-1, keepdims=True)
    shift = mod_ref[:, 3 * d:4 * d]
    scale = mod_ref[:, 4 * d:5 * d]
    gate = mod_ref[:, 5 * d:6 * d]
    h = (x * lax.rsqrt(ms + EPS) * n1_ref[...] * (1.0 + scale) + shift).astype(BF16)
    acc = None
    for c in range(w1_ref.shape[1] // ff_chunk):
        sl = slice(c * ff_chunk, (c + 1) * ff_chunk)
        u = jnp.maximum(_dot(h, w1_ref[:, sl]), 0.0)
        part = _dot((u * u).astype(BF16), w2_ref[sl, :])
        acc = part if acc is None else acc + part
    ms2 = jnp.mean(acc * acc, axis=-1, keepdims=True)
    out_ref[...] = x + gate * (acc * lax.rsqrt(ms2 + EPS) * n2_ref[...])


def _mlp(x, mod3, n1, n2, w1, w2, tm, ff_chunk):
    bsz, s, d = x.shape
    ff = w1.shape[1]
    tok = lambda b, t: (b, t, 0)
    const = lambda b, t: (0, 0)
    kern = functools.partial(_mlp_kernel, ff_chunk=ff_chunk)
    return pl.pallas_call(
        kern,
        grid=(bsz, s // tm),
        in_specs=[pl.BlockSpec((None, tm, d), tok),
                  pl.BlockSpec((None, 1, mod3.shape[-1]), lambda b, t: (b, 0, 0)),
                  pl.BlockSpec((1, d), const),
                  pl.BlockSpec((1, d), const),
                  pl.BlockSpec((d, ff), const, pipeline_mode=pl.Buffered(1)),
                  pl.BlockSpec((ff, d), const, pipeline_mode=pl.Buffered(1))],
        out_specs=pl.BlockSpec((None, tm, d), tok),
        out_shape=jax.ShapeDtypeStruct((bsz, s, d), F32),
        compiler_params=_params("parallel", "parallel"),
        name="mlp",
    )(x, mod3, n1, n2, w1, w2)


def _layer(x, mod, norm1_pre, norm1_post, w_in, b_gates, conv_w, conv_b, mlstm_gn, attn_qnorm,
           attn_knorm, w_branch_m, w_branch_a, w_out, norm2_pre, norm2_post, w_mlp_in, w_mlp_out):
    bsz, s, d = x.shape
    inner = d
    n_gate = 4 * MLSTM_HEADS
    q_inner = d
    kv_inner = ATTN_KV_HEADS * ATTN_HEAD_DIM

    o_g = 4 * inner
    o_qa = o_g + n_gate
    o_ka = o_qa + q_inner
    o_va = o_ka + kv_inner
    o_br = o_va + kv_inner
    perm = _rope_head_perm()
    head_perm = lambda w: w.reshape(d, -1, ATTN_HEAD_DIM)[:, :, perm].reshape(d, -1)
    w_main = jnp.concatenate([w_in[:, 0:o_g], w_in[:, o_br:o_br + 2 * d], head_perm(w_in[:, o_qa:o_va]),
                              w_in[:, o_va:o_br]], axis=1).astype(BF16)
    conv_cols, mid_cols = 2 * inner, inner + 2 * d
    hm = MLSTM_HEADS
    wg = w_in[:, o_g:o_qa]
    gate_cols = lambda a: (jnp.concatenate([a[..., 0:hm], a[..., 2 * hm:3 * hm]], -1),
                           jnp.concatenate([a[..., hm:2 * hm], a[..., 3 * hm:4 * hm]], -1))
    wi, wf = gate_cols(wg)
    bi, bf = gate_cols(b_gates)
    lane_pad = lambda a: jnp.pad(a, [(0, 0)] * (a.ndim - 1) + [(0, LANES - a.shape[-1])])
    w_gates = jnp.concatenate([lane_pad(wi), lane_pad(wf)], -1).astype(BF16)
    bias_gates = jnp.concatenate([lane_pad(bi), lane_pad(bf)], -1).reshape(1, 2 * LANES)

    mod3 = mod.reshape(bsz, 1, mod.shape[-1])
    cos, sin = _rope_tables(s)
    dh_m = inner // MLSTM_HEADS
    qk, vt_m, mid, q_rot, k_rot, v_t, gates = _in_proj(
        x, mod3, norm1_pre.reshape(1, d), w_main, w_gates, bias_gates, conv_w, conv_b, cos, sin,
        attn_qnorm[perm], attn_knorm[perm], conv_cols=conv_cols, mid_cols=mid_cols, qa_cols=q_inner,
        ka_cols=kv_inner, tm=min(s, 512), q_scale_m=dh_m ** -0.5,
        q_scale_a=ATTN_HEAD_DIM ** -0.5 * math.log2(math.e))

    h_f, h_b = _mlstm(qk, vt_m, gates, ln=min(s, 256))
    y_a = _attention(q_rot, k_rot, v_t, tq=min(s, 1024))

    x1 = _merge(h_f, h_b, mid, y_a, x, mod3, mlstm_gn.reshape(1, d), norm1_post.reshape(1, d),
                w_branch_m.astype(BF16), w_branch_a.astype(BF16), w_out.astype(BF16),
                o_block=0, bm_block=1, ba_block=2, tm=min(s, 512))
    return _mlp(x1, mod3, norm2_pre.reshape(1, d), norm2_post.reshape(1, d),
                w_mlp_in.astype(BF16), w_mlp_out.astype(BF16), tm=min(s, 512), ff_chunk=1024)


def kernel(x, c, w_ada, b_ada, norm1_pre, norm1_post, w_in, b_gates, conv_w, conv_b, mlstm_gn, attn_qnorm, attn_knorm, w_branch_m, w_branch_a, w_out, norm2_pre, norm2_post, w_mlp_in, w_mlp_out):
    for l in range(w_ada.shape[0]):
        mod = _ada(c, w_ada[l], b_ada[l])
        x = _layer(x, mod, norm1_pre[l], norm1_post[l], w_in[l], b_gates[l], conv_w[l], conv_b[l],
                   mlstm_gn[l], attn_qnorm[l], attn_knorm[l], w_branch_m[l], w_branch_a[l], w_out[l],
                   norm2_pre[l], norm2_post[l], w_mlp_in[l], w_mlp_out[l])
    return x
```

```python
import functools
import math

import jax
import jax.numpy as jnp
from jax import lax
from jax.experimental import pallas as pl
from jax.experimental.pallas import tpu as pltpu

EPS = 1e-6
GRID_W = 64
ROPE_THETA = 10000.0
MLSTM_HEADS = 4
MLSTM_CONV_W = 5
ATTN_HEAD_DIM = 128
ATTN_KV_HEADS = 2
LANES = 128
HALO = 16
ONES_ROWS = 16

F32 = jnp.float32
BF16 = jnp.bfloat16
VMEM_LIMIT = 56 * 1024 * 1024


def _params(*sem):
    return pltpu.CompilerParams(dimension_semantics=sem, vmem_limit_bytes=VMEM_LIMIT)


def _sigmoid(x):
    return 0.5 * jnp.tanh(0.5 * x) + 0.5


def _silu(x):
    half = 0.5 * x
    return half * jnp.tanh(half) + half


def _dot(a, b):
    return jnp.dot(a, b, preferred_element_type=F32)


def _dot_nt(a, b):
    return lax.dot_general(a, b, (((1,), (1,)), ((), ())), preferred_element_type=F32)


def _ada_kernel(c_ref, w_ref, b_ref, o_ref):
    c = c_ref[...]
    sc = _silu(c)
    o_ref[...] = jnp.dot(sc, w_ref[...], preferred_element_type=F32,
                         precision=lax.Precision.HIGHEST) + b_ref[...]


def _ada(c, w, b):
    bsz, d = c.shape
    n = w.shape[1]
    tn = 1024
    return pl.pallas_call(
        _ada_kernel,
        grid=(n // tn,),
        in_specs=[pl.BlockSpec((bsz, d), lambda j: (0, 0)),
                  pl.BlockSpec((d, tn), lambda j: (0, j)),
                  pl.BlockSpec((1, tn), lambda j: (0, j))],
        out_specs=pl.BlockSpec((bsz, tn), lambda j: (0, j)),
        out_shape=jax.ShapeDtypeStruct((bsz, n), F32),
        compiler_params=_params("parallel"),
        name="ada",
    )(c, w, b.reshape(1, n))


def _rope_head_perm():
    quarter = ATTN_HEAD_DIM // 4
    blocks = [0, 2, 1, 3]
    return jnp.concatenate([jnp.arange(b * quarter, (b + 1) * quarter) for b in blocks])


def _norm_rope(x, w, cos, sin):
    ms = jnp.mean(x * x, axis=-1, keepdims=True)
    y = x * lax.rsqrt(ms + EPS) * w
    return y * cos + pltpu.roll(y, LANES // 2, 1) * sin


def _in_proj_kernel(xp_ref, x_ref, xn_ref, mod_ref, nw_ref, w_ref, wg_ref, bg_ref, cw_ref, cb_ref,
                    cos_ref, sin_ref, qw_ref, kw_ref,
                    qk_ref, vm_ref, mid_ref, qo_ref, ko_ref, vt_ref, g_ref, h_scr, e_scr,
                    *, chunk, q_scale_m, q_scale_a):
    t = pl.program_id(1)
    nt = pl.num_programs(1)
    tm, d = x_ref.shape
    conv_cols = qk_ref.shape[1]
    dh_m = conv_cols // (2 * MLSTM_HEADS)
    mid_cols = mid_ref.shape[1]
    qa_cols = qo_ref.shape[1]
    ka_cols = ko_ref.shape[1]
    gain = nw_ref[...] * (1.0 + mod_ref[:, d:2 * d])
    shift = mod_ref[:, 0:d]

    def normed(xv):
        ms = jnp.mean(xv * xv, axis=-1, keepdims=True)
        return (xv * lax.rsqrt(ms + EPS) * gain + shift).astype(BF16)

    h_scr[0:HALO, :] = normed(xp_ref[...])
    h_scr[HALO:HALO + tm, :] = normed(x_ref[...])
    h_scr[HALO + tm:, :] = normed(xn_ref[...])
    centre = slice(HALO, HALO + tm)
    g_ref[...] = _dot(h_scr[centre, :], wg_ref[...]) + bg_ref[...]

    half = HALO // 2
    pad = MLSTM_CONV_W // 2
    has_prev = (t > 0).astype(F32)
    has_next = (t < nt - 1).astype(F32)

    def conv_chunk(c0):
        cs = slice(c0, c0 + chunk)
        e = _dot(h_scr[...], w_ref[:, cs])
        e_scr[0:half, :] = e[half:HALO] * has_prev
        e_scr[half:half + tm, :] = e[HALO:HALO + tm]
        e_scr[half + tm:, :] = e[HALO + tm:HALO + tm + half] * has_next
        acc = None
        for j in range(MLSTM_CONV_W):
            term = e_scr[half - pad + j:half - pad + j + tm, :] * cw_ref[j:j + 1, cs]
            acc = term if acc is None else acc + term
        y = acc + cb_ref[:, cs]
        y = _silu(y)
        if c0 < conv_cols // 2:
            y = y * q_scale_m
        qk_ref[:, cs] = y.astype(qk_ref.dtype)

    vm_base = conv_cols
    vm_cols = (vm_ref.shape[0] // (dh_m + ONES_ROWS)) * dh_m

    def vm_chunk(c0):
        a = _dot(h_scr[centre, :], w_ref[:, vm_base + c0:vm_base + c0 + chunk]).astype(vm_ref.dtype)
        for hh in range(chunk // dh_m):
            r0 = (c0 // dh_m + hh) * (dh_m + ONES_ROWS)
            for c in range(0, dh_m, LANES):
                vm_ref[r0 + c:r0 + c + LANES, :] = a[:, hh * dh_m + c:hh * dh_m + c + LANES].T
            vm_ref[r0 + dh_m:r0 + dh_m + ONES_ROWS, :] = jnp.ones((ONES_ROWS, tm), vm_ref.dtype)

    mid_base = vm_base + vm_cols

    def mid_chunk(c0):
        mid_ref[:, c0:c0 + chunk] = _dot(h_scr[centre, :],
                                         w_ref[:, mid_base + c0:mid_base + c0 + chunk]).astype(mid_ref.dtype)

    qa_base = mid_base + mid_cols
    cos = cos_ref[...]
    sin = sin_ref[...]

    def q_chunk(c0):
        a = _dot(h_scr[centre, :], w_ref[:, qa_base + c0:qa_base + c0 + chunk])
        for hh in range(chunk // LANES):
            y = _norm_rope(a[:, hh * LANES:(hh + 1) * LANES], qw_ref[...], cos, sin)
            qo_ref[:, c0 + hh * LANES:c0 + (hh + 1) * LANES] = (y * q_scale_a).astype(qo_ref.dtype)

    kv_base = qa_base + qa_cols

    def kv_chunk():
        a = _dot(h_scr[centre, :], w_ref[:, kv_base:kv_base + 2 * ka_cols])
        for g in range(ka_cols // LANES):
            y = _norm_rope(a[:, g * LANES:(g + 1) * LANES], kw_ref[...], cos, sin)
            ko_ref[:, g * LANES:(g + 1) * LANES] = y.astype(ko_ref.dtype)
            r0 = g * (LANES + ONES_ROWS)
            v = a[:, ka_cols + g * LANES:ka_cols + (g + 1) * LANES].astype(vt_ref.dtype)
            vt_ref[r0:r0 + LANES, :] = v.T
            vt_ref[r0 + LANES:r0 + LANES + ONES_ROWS, :] = jnp.ones((ONES_ROWS, tm), vt_ref.dtype)

    heavy = ([functools.partial(conv_chunk, c0) for c0 in range(0, conv_cols, chunk)]
             + [functools.partial(q_chunk, c0) for c0 in range(0, qa_cols, chunk)] + [kv_chunk])
    light = ([functools.partial(vm_chunk, c0) for c0 in range(0, vm_cols, chunk)]
             + [functools.partial(mid_chunk, c0) for c0 in range(0, mid_cols, chunk)])
    for i in range(max(len(heavy), len(light))):
        for task in heavy[i:i + 1] + light[i:i + 1]:
            task()


def _in_proj(x, mod3, norm_w, w_main, w_gates, b_gates, conv_w, conv_b, cos, sin, qn_w, kn_w,
             conv_cols, mid_cols, qa_cols, ka_cols, tm, q_scale_m, q_scale_a):
    bsz, s, d = x.shape
    n = w_main.shape[1]
    vm_cols = conv_cols // 2
    assert n == conv_cols + vm_cols + mid_cols + qa_cols + 2 * ka_cols
    ng = w_gates.shape[1]
    vm_rows = MLSTM_HEADS * (vm_cols // MLSTM_HEADS + ONES_ROWS)
    vt_rows = (ka_cols // LANES) * (LANES + ONES_ROWS)
    nh = s // HALO
    chunk = 512
    kern = functools.partial(_in_proj_kernel, chunk=chunk, q_scale_m=q_scale_m, q_scale_a=q_scale_a)
    tok = lambda b, t: (b, t, 0)
    const = lambda b, t: (0, 0)
    bf = lambda cols: jax.ShapeDtypeStruct((bsz, s, cols), BF16)
    return pl.pallas_call(
        kern,
        grid=(bsz, s // tm),
        in_specs=[pl.BlockSpec((None, HALO, d), lambda b, t: (b, jnp.maximum(t * (tm // HALO) - 1, 0), 0)),
                  pl.BlockSpec((None, tm, d), tok),
                  pl.BlockSpec((None, HALO, d),
                               lambda b, t: (b, jnp.minimum((t + 1) * (tm // HALO), nh - 1), 0)),
                  pl.BlockSpec((None, 1, mod3.shape[-1]), lambda b, t: (b, 0, 0)),
                  pl.BlockSpec((1, d), const),
                  pl.BlockSpec((d, n), const, pipeline_mode=pl.Buffered(1)),
                  pl.BlockSpec((d, ng), const),
                  pl.BlockSpec((1, ng), const),
                  pl.BlockSpec((MLSTM_CONV_W, conv_cols), const),
                  pl.BlockSpec((1, conv_cols), const),
                  pl.BlockSpec((tm, LANES), lambda b, t: (t, 0)),
                  pl.BlockSpec((tm, LANES), lambda b, t: (t, 0)),
                  pl.BlockSpec((1, LANES), const),
                  pl.BlockSpec((1, LANES), const)],
        out_specs=[pl.BlockSpec((None, tm, conv_cols), tok),
                   pl.BlockSpec((None, None, vm_rows, tm), lambda b, t: (b, t, 0, 0)),
                   pl.BlockSpec((None, tm, mid_cols), tok),
                   pl.BlockSpec((None, tm, qa_cols), tok),
                   pl.BlockSpec((None, tm, ka_cols), tok),
                   pl.BlockSpec((None, None, vt_rows, tm), lambda b, t: (b, t, 0, 0)),
                   pl.BlockSpec((None, tm, ng), tok)],
        out_shape=[bf(conv_cols),
                   jax.ShapeDtypeStruct((bsz, s // tm, vm_rows, tm), BF16),
                   bf(mid_cols), bf(qa_cols), bf(ka_cols),
                   jax.ShapeDtypeStruct((bsz, s // tm, vt_rows, tm), BF16),
                   jax.ShapeDtypeStruct((bsz, s, ng), F32)],
        scratch_shapes=[pltpu.VMEM((tm + 2 * HALO, d), BF16),
                        pltpu.VMEM((tm + HALO, chunk), F32)],
        compiler_params=_params("parallel", "parallel"),
        name="in_proj",
    )(x, x, x, mod3, norm_w, w_main, w_gates, b_gates, conv_w, conv_b.reshape(1, conv_cols), cos, sin,
      qn_w.reshape(1, LANES), kn_w.reshape(1, LANES))


def _mlstm_kernel(qkf_ref, qkb_ref, vtf_ref, vtb_ref, gf_ref, gb_ref, hf_ref, hb_ref, st_scr, m_scr):
    ln = qkf_ref.shape[0]
    inner = qkf_ref.shape[1] // 2
    dh = inner // MLSTM_HEADS
    vrows = dh + ONES_ROWS

    @pl.when(pl.program_id(1) == 0)
    def _():
        st_scr[...] = jnp.zeros_like(st_scr)
        m_scr[...] = jnp.full_like(m_scr, -jnp.inf)

    row = lax.broadcasted_iota(jnp.int32, (ln, ln), 0)
    col = lax.broadcasted_iota(jnp.int32, (ln, ln), 1)
    lower = col <= row
    upper = col >= row

    def gate_terms(g_ref, tri):
        gi = g_ref[:, 0:LANES]
        gf = g_ref[:, LANES:2 * LANES]
        a = jnp.minimum(gf, 0.0) - jnp.log(1.0 + jnp.exp(-jnp.abs(gf)))
        tri_b = jnp.where(tri, 1.0, 0.0).astype(BF16)
        a1 = a.astype(BF16)
        r1 = a - a1.astype(F32)
        a2 = r1.astype(BF16)
        a3 = (r1 - a2.astype(F32)).astype(BF16)
        b = _dot(tri_b, a1) + _dot(tri_b, a2) + _dot(tri_b, a3)
        return gi - b, b.T

    u_f, bt_f = gate_terms(gf_ref, lower)
    u_b, bt_b = gate_terms(gb_ref, upper)

    for direction in range(2):
        qk_ref, vt_ref, h_ref = ((qkf_ref, vtf_ref, hf_ref), (qkb_ref, vtb_ref, hb_ref))[direction]
        umat, btmat = ((u_f, bt_f), (u_b, bt_b))[direction]
        mask = (upper, lower)[direction]
        for head in range(MLSTM_HEADS):
            j = direction * MLSTM_HEADS + head
            q = qk_ref[:, head * dh:(head + 1) * dh]
            k = qk_ref[:, inner + head * dh:inner + (head + 1) * dh]
            vt = vt_ref[head * vrows:(head + 1) * vrows, :]
            u_col = umat[:, j:j + 1]
            b_row = btmat[j:j + 1, :]
            m_prev = m_scr[j:j + 1, 0:1]
            state = st_scr[j]

            u_m = jnp.where(mask, jnp.broadcast_to(u_col, (ln, ln)), -jnp.inf)
            g_row = jnp.maximum(jnp.max(u_m, axis=0, keepdims=True), m_prev)
            w_t = _dot_nt(k, q) * jnp.exp(u_m - g_row)
            inter = jnp.exp(m_prev - g_row)
            tot = _dot(vt, w_t.astype(BF16)) + inter * _dot_nt(state.astype(BF16), q)
            floor = jnp.exp(-(g_row + b_row))
            h_t = tot[0:dh, :] / jnp.maximum(jnp.abs(tot[dh:dh + 1, :]), floor)
            h_ref[:, head * dh:(head + 1) * dh] = h_t.T

            g_end = jnp.max(g_row, axis=1, keepdims=True)
            b_tot = jnp.min(b_row, axis=1, keepdims=True)
            kw = k.astype(F32) * jnp.exp(jnp.broadcast_to(u_col, (ln, dh)) - g_end)
            st_scr[j] = jnp.exp(m_prev - g_end) * state + _dot(vt, kw.astype(BF16))
            m_scr[j:j + 1, :] = jnp.broadcast_to(b_tot + g_end, (1, m_scr.shape[1]))


def _mlstm(qk, vt, gates, ln):
    bsz, s, two_inner = qk.shape
    inner = two_inner // 2
    nc = s // ln
    ng = gates.shape[-1]
    dh = inner // MLSTM_HEADS
    nchain = 2 * MLSTM_HEADS
    vt_rows, tv = vt.shape[2], vt.shape[3]
    per = tv // ln
    fwd = lambda b, i: (b, i, 0)
    bwd = lambda b, i: (b, nc - 1 - i, 0)
    return pl.pallas_call(
        _mlstm_kernel,
        grid=(bsz, nc),
        in_specs=[pl.BlockSpec((None, ln, two_inner), fwd),
                  pl.BlockSpec((None, ln, two_inner), bwd),
                  pl.BlockSpec((None, None, vt_rows, ln), lambda b, i: (b, i // per, 0, i % per)),
                  pl.BlockSpec((None, None, vt_rows, ln),
                               lambda b, i: (b, (nc - 1 - i) // per, 0, (nc - 1 - i) % per)),
                  pl.BlockSpec((None, ln, ng), fwd),
                  pl.BlockSpec((None, ln, ng), bwd)],
        out_specs=[pl.BlockSpec((None, ln, inner), fwd),
                   pl.BlockSpec((None, ln, inner), bwd)],
        out_shape=[jax.ShapeDtypeStruct((bsz, s, inner), F32),
                   jax.ShapeDtypeStruct((bsz, s, inner), F32)],
        scratch_shapes=[pltpu.VMEM((nchain, dh + ONES_ROWS, dh), F32),
                        pltpu.VMEM((nchain, LANES), F32)],
        compiler_params=_params("parallel", "arbitrary"),
        name="mlstm",
    )(qk, qk, vt, vt, gates, gates)


def _rope_tab_kernel(freq_ref, cos_ref, sin_ref):
    ts = cos_ref.shape[0]
    t = pl.program_id(0) * ts + lax.broadcasted_iota(jnp.int32, (ts, LANES), 0)
    lane = lax.broadcasted_iota(jnp.int32, (ts, LANES), 1)
    grid_shift = GRID_W.bit_length() - 1
    is_col = (lane & (LANES // 4)) != 0
    pos = jnp.where(is_col, t & (GRID_W - 1), t >> grid_shift).astype(F32)
    ang = pos * freq_ref[...]
    sign = jnp.where(lane < LANES // 2, -1.0, 1.0)
    cos_ref[...] = jnp.cos(ang)
    sin_ref[...] = jnp.sin(ang) * sign


def _rope_tables(s):
    n_freq = ATTN_HEAD_DIM // 4
    freqs = ROPE_THETA ** (-jnp.arange(n_freq, dtype=F32) / n_freq)
    freq_lanes = jnp.tile(freqs, 4).reshape(1, ATTN_HEAD_DIM)
    ts = min(s, 1024)
    return pl.pallas_call(
        _rope_tab_kernel,
        grid=(s // ts,),
        in_specs=[pl.BlockSpec((1, LANES), lambda t: (0, 0))],
        out_specs=[pl.BlockSpec((ts, LANES), lambda t: (t, 0)),
                   pl.BlockSpec((ts, LANES), lambda t: (t, 0))],
        out_shape=[jax.ShapeDtypeStruct((s, LANES), F32),
                   jax.ShapeDtypeStruct((s, LANES), F32)],
        compiler_params=_params("parallel"),
        name="rope_tab",
    )(freq_lanes)


def _attn_kernel(q_ref, k_ref, vt_ref, o_ref, qs_scr, m_scr, alpha_scr, acc_scr, st_scr, p_scr):
    tq = q_ref.shape[0]
    group = q_ref.shape[1] // LANES
    nk, tk = k_ref.shape[0], k_ref.shape[1]
    dh = LANES
    strip = 128

    for h in range(group):
        qs_scr[:, h * tq:(h + 1) * tq] = q_ref[:, h * LANES:(h + 1) * LANES].T
    m_scr[...] = jnp.full_like(m_scr, -jnp.inf)
    acc_scr[...] = jnp.zeros_like(acc_scr)

    def scores(j, slot):
        st_scr[slot] = _dot(k_ref[j], qs_scr[...]).astype(BF16)

    def softmax(slot):
        for h in range(group):
            cols = slice(h * tq, (h + 1) * tq)
            m_old = m_scr[:, cols]
            cmax = jnp.max(st_scr[slot, :, cols], axis=0, keepdims=True).astype(F32)
            m_new = jnp.maximum(m_old, cmax)
            m_b = m_new.astype(BF16)
            for r in range(0, tk, strip):
                p_scr[slot, r:r + strip, cols] = jnp.exp2(st_scr[slot, r:r + strip, cols] - m_b)
            alpha_scr[slot, :, cols] = jnp.exp2(m_old - m_new)
            m_scr[:, cols] = m_new

    def pv(j, slot):
        acc_scr[...] = alpha_scr[slot] * acc_scr[...] + _dot(vt_ref[j], p_scr[slot])

    assert nk % 2 == 0
    scores(0, 0)
    scores(1, 1)
    softmax(0)

    def body(i, carry):
        j = 2 * i + 1
        scores(j + 1, 0)
        softmax(1)
        pv(j - 1, 0)
        scores(j + 2, 1)
        softmax(0)
        pv(j, 1)
        return carry

    lax.fori_loop(0, (nk - 2) // 2, body, 0)
    softmax(1)
    pv(nk - 2, 0)
    pv(nk - 1, 1)
    for h in range(group):
        cols = slice(h * tq, (h + 1) * tq)
        out_t = acc_scr[0:dh, cols] / acc_scr[dh:dh + 1, cols]
        o_ref[:, h * LANES:(h + 1) * LANES] = out_t.T.astype(o_ref.dtype)


def _attention(q, k, vt, tq):
    bsz, s, q_inner = q.shape
    dh = ATTN_HEAD_DIM
    nk, tk = vt.shape[1], vt.shape[3]
    kvh = k.shape[-1] // dh
    gw = q_inner // kvh
    group = gw // dh
    vrows = dh + ONES_ROWS
    k4 = k.reshape(bsz, nk, tk, kvh * dh)
    return pl.pallas_call(
        _attn_kernel,
        grid=(bsz, kvh, s // tq),
        in_specs=[pl.BlockSpec((None, tq, gw), lambda b, g, i: (b, i, g)),
                  pl.BlockSpec((None, nk, tk, dh), lambda b, g, i: (b, 0, 0, g)),
                  pl.BlockSpec((None, nk, vrows, tk), lambda b, g, i: (b, 0, g, 0))],
        out_specs=pl.BlockSpec((None, tq, gw), lambda b, g, i: (b, i, g)),
        out_shape=jax.ShapeDtypeStruct((bsz, s, q_inner), BF16),
        scratch_shapes=[pltpu.VMEM((dh, group * tq), BF16),
                        pltpu.VMEM((1, group * tq), F32),
                        pltpu.VMEM((2, 1, group * tq), F32),
                        pltpu.VMEM((vrows, group * tq), F32),
                        pltpu.VMEM((2, tk, group * tq), BF16),
                        pltpu.VMEM((2, tk, group * tq), BF16)],
        compiler_params=_params("parallel", "parallel", "parallel"),
        name="attn",
    )(q, k4, vt)


def _post_kernel(hf_ref, hb_ref, o_ref, bm_ref, ba_ref, ya_ref, x_ref, mod_ref, gn_ref, n1_ref, wm_ref,
                 wa_ref, wo_ref, n2a_ref, n2b_ref, w1_ref, w2_ref, out_ref, ym_scr, x1_scr, *, ff_chunk):
    d = x_ref.shape[-1]
    dh = d // MLSTM_HEADS
    for head in range(MLSTM_HEADS):
        sl = slice(head * dh, (head + 1) * dh)
        h = hf_ref[:, sl] + hb_ref[:, sl]
        mu = jnp.mean(h, axis=-1, keepdims=True)
        hc = h - mu
        var = jnp.mean(hc * hc, axis=-1, keepdims=True)
        hn = hc * lax.rsqrt(var + EPS) * gn_ref[:, sl]
        ym_scr[:, sl] = (hn * _sigmoid(o_ref[:, sl].astype(F32))).astype(BF16)
    pm = _dot(ym_scr[...], wm_ref[...])
    pa = _dot(ya_ref[...], wa_ref[...])
    y = _sigmoid(bm_ref[...].astype(F32)) * pm + _sigmoid(ba_ref[...].astype(F32)) * pa
    y2 = _dot(y.astype(BF16), wo_ref[...])
    ms = jnp.mean(y2 * y2, axis=-1, keepdims=True)
    x1_scr[...] = x_ref[...] + mod_ref[:, 2 * d:3 * d] * (y2 * lax.rsqrt(ms + EPS) * n1_ref[...])

    x1 = x1_scr[...]
    ms1 = jnp.mean(x1 * x1, axis=-1, keepdims=True)
    shift = mod_ref[:, 3 * d:4 * d]
    scale = mod_ref[:, 4 * d:5 * d]
    gate = mod_ref[:, 5 * d:6 * d]
    h2 = (x1 * lax.rsqrt(ms1 + EPS) * n2a_ref[...] * (1.0 + scale) + shift).astype(BF16)
    acc = None
    for c in range(w1_ref.shape[1] // ff_chunk):
        sl = slice(c * ff_chunk, (c + 1) * ff_chunk)
        u = jnp.maximum(_dot(h2, w1_ref[:, sl]), 0.0)
        part = _dot((u * u).astype(BF16), w2_ref[sl, :])
        acc = part if acc is None else acc + part
    ms2 = jnp.mean(acc * acc, axis=-1, keepdims=True)
    out_ref[...] = x1_scr[...] + gate * (acc * lax.rsqrt(ms2 + EPS) * n2b_ref[...])


def _post(hf, hb, mid, ya, x, mod3, gn_w, n1_post, wm, wa, wo, n2_pre, n2_post, w1, w2,
          o_block, bm_block, ba_block, tm, ff_chunk):
    bsz, s, d = x.shape
    ff = w1.shape[1]
    tok = lambda b, t: (b, t, 0)
    const = lambda b, t: (0, 0)
    resident = lambda shape: pl.BlockSpec(shape, const, pipeline_mode=pl.Buffered(1))
    kern = functools.partial(_post_kernel, ff_chunk=ff_chunk)
    return pl.pallas_call(
        kern,
        grid=(bsz, s // tm),
        in_specs=[pl.BlockSpec((None, tm, d), tok),
                  pl.BlockSpec((None, tm, d), tok),
                  pl.BlockSpec((None, tm, d), lambda b, t: (b, t, o_block)),
                  pl.BlockSpec((None, tm, d), lambda b, t: (b, t, bm_block)),
                  pl.BlockSpec((None, tm, d), lambda b, t: (b, t, ba_block)),
                  pl.BlockSpec((None, tm, d), tok),
                  pl.BlockSpec((None, tm, d), tok),
                  pl.BlockSpec((None, 1, mod3.shape[-1]), lambda b, t: (b, 0, 0)),
                  pl.BlockSpec((1, d), const),
                  pl.BlockSpec((1, d), const),
                  resident((d, d)), resident((d, d)), resident((d, d)),
                  pl.BlockSpec((1, d), const),
                  pl.BlockSpec((1, d), const),
                  resident((d, ff)), resident((ff, d))],
        out_specs=pl.BlockSpec((None, tm, d), tok),
        out_shape=jax.ShapeDtypeStruct((bsz, s, d), F32),
        scratch_shapes=[pltpu.VMEM((tm, d), BF16), pltpu.VMEM((tm, d), F32)],
        compiler_params=_params("parallel", "parallel"),
        name="post",
    )(hf, hb, mid, mid, mid, ya, x, mod3, gn_w, n1_post, wm, wa, wo, n2_pre, n2_post, w1, w2)


def _layer(x, mod, norm1_pre, norm1_post, w_in, b_gates, conv_w, conv_b, mlstm_gn, attn_qnorm,
           attn_knorm, w_branch_m, w_branch_a, w_out, norm2_pre, norm2_post, w_mlp_in, w_mlp_out):
    bsz, s, d = x.shape
    inner = d
    n_gate = 4 * MLSTM_HEADS
    q_inner = d
    kv_inner = ATTN_KV_HEADS * ATTN_HEAD_DIM

    o_g = 4 * inner
    o_qa = o_g + n_gate
    o_ka = o_qa + q_inner
    o_va = o_ka + kv_inner
    o_br = o_va + kv_inner
    perm = _rope_head_perm()
    head_perm = lambda w: w.reshape(d, -1, ATTN_HEAD_DIM)[:, :, perm].reshape(d, -1)
    w_main = jnp.concatenate([w_in[:, 0:o_g], w_in[:, o_br:o_br + 2 * d], head_perm(w_in[:, o_qa:o_va]),
                              w_in[:, o_va:o_br]], axis=1).astype(BF16)
    conv_cols, mid_cols = 2 * inner, inner + 2 * d
    hm = MLSTM_HEADS
    wg = w_in[:, o_g:o_qa]
    gate_cols = lambda a: (jnp.concatenate([a[..., 0:hm], a[..., 2 * hm:3 * hm]], -1),
                           jnp.concatenate([a[..., hm:2 * hm], a[..., 3 * hm:4 * hm]], -1))
    wi, wf = gate_cols(wg)
    bi, bf = gate_cols(b_gates)
    lane_pad = lambda a: jnp.pad(a, [(0, 0)] * (a.ndim - 1) + [(0, LANES - a.shape[-1])])
    w_gates = jnp.concatenate([lane_pad(wi), lane_pad(wf)], -1).astype(BF16)
    bias_gates = jnp.concatenate([lane_pad(bi), lane_pad(bf)], -1).reshape(1, 2 * LANES)

    mod3 = mod.reshape(bsz, 1, mod.shape[-1])
    cos, sin = _rope_tables(s)
    dh_m = inner // MLSTM_HEADS
    qk, vt_m, mid, q_rot, k_rot, v_t, gates = _in_proj(
        x, mod3, norm1_pre.reshape(1, d), w_main, w_gates, bias_gates, conv_w, conv_b, cos, sin,
        attn_qnorm[perm], attn_knorm[perm], conv_cols=conv_cols, mid_cols=mid_cols, qa_cols=q_inner,
        ka_cols=kv_inner, tm=min(s, 512), q_scale_m=dh_m ** -0.5,
        q_scale_a=ATTN_HEAD_DIM ** -0.5 * math.log2(math.e))

    h_f, h_b = _mlstm(qk, vt_m, gates, ln=min(s, 256))
    y_a = _attention(q_rot, k_rot, v_t, tq=min(s, 1024))

    return _post(h_f, h_b, mid, y_a, x, mod3, mlstm_gn.reshape(1, d), norm1_post.reshape(1, d),
                 w_branch_m.astype(BF16), w_branch_a.astype(BF16), w_out.astype(BF16),
                 norm2_pre.reshape(1, d), norm2_post.reshape(1, d), w_mlp_in.astype(BF16),
                 w_mlp_out.astype(BF16), o_block=0, bm_block=1, ba_block=2, tm=min(s, 512), ff_chunk=1024)


def kernel(x, c, w_ada, b_ada, norm1_pre, norm1_post, w_in, b_gates, conv_w, conv_b, mlstm_gn, attn_qnorm, attn_knorm, w_branch_m, w_branch_a, w_out, norm2_pre, norm2_post, w_mlp_in, w_mlp_out):
    for l in range(w_ada.shape[0]):
        mod = _ada(c, w_ada[l], b_ada[l])
        x = _layer(x, mod, norm1_pre[l], norm1_post[l], w_in[l], b_gates[l], conv_w[l], conv_b[l],
                   mlstm_gn[l], attn_qnorm[l], attn_knorm[l], w_branch_m[l], w_branch_a[l], w_out[l],
                   norm2_pre[l], norm2_post[l], w_mlp_in[l], w_mlp_out[l])
    return x
```

```python
import functools
import math

import jax
import jax.numpy as jnp
from jax import lax
from jax.experimental import pallas as pl
from jax.experimental.pallas import tpu as pltpu

EPS = 1e-6
GRID_W = 64
ROPE_THETA = 10000.0
MLSTM_HEADS = 4
MLSTM_CONV_W = 5
ATTN_HEAD_DIM = 128
ATTN_KV_HEADS = 2
LANES = 128
HALO = 16
ONES_ROWS = 16

F32 = jnp.float32
BF16 = jnp.bfloat16
VMEM_LIMIT = 56 * 1024 * 1024


def _params(*sem):
    return pltpu.CompilerParams(dimension_semantics=sem, vmem_limit_bytes=VMEM_LIMIT)


def _sigmoid(x):
    return 0.5 * jnp.tanh(0.5 * x) + 0.5


def _silu(x):
    half = 0.5 * x
    return half * jnp.tanh(half) + half


def _dot(a, b):
    return jnp.dot(a, b, preferred_element_type=F32)


def _dot_nt(a, b):
    return lax.dot_general(a, b, (((1,), (1,)), ((), ())), preferred_element_type=F32)


def _ada_kernel(c_ref, w_ref, b_ref, o_ref):
    c = c_ref[...]
    sc = _silu(c)
    o_ref[...] = jnp.dot(sc, w_ref[...], preferred_element_type=F32,
                         precision=lax.Precision.HIGHEST) + b_ref[...]


def _ada(c, w, b):
    bsz, d = c.shape
    n = w.shape[1]
    tn = 1024
    return pl.pallas_call(
        _ada_kernel,
        grid=(n // tn,),
        in_specs=[pl.BlockSpec((bsz, d), lambda j: (0, 0)),
                  pl.BlockSpec((d, tn), lambda j: (0, j)),
                  pl.BlockSpec((1, tn), lambda j: (0, j))],
        out_specs=pl.BlockSpec((bsz, tn), lambda j: (0, j)),
        out_shape=jax.ShapeDtypeStruct((bsz, n), F32),
        compiler_params=_params("parallel"),
        name="ada",
    )(c, w, b.reshape(1, n))


def _rope_head_perm():
    quarter = ATTN_HEAD_DIM // 4
    blocks = [0, 2, 1, 3]
    return jnp.concatenate([jnp.arange(b * quarter, (b + 1) * quarter) for b in blocks])


def _norm_rope(x, w, cos, sin):
    ms = jnp.mean(x * x, axis=-1, keepdims=True)
    y = x * lax.rsqrt(ms + EPS) * w
    return y * cos + pltpu.roll(y, LANES // 2, 1) * sin


def _in_proj_kernel(xp_ref, x_ref, xn_ref, mod_ref, nw_ref, w_ref, wg_ref, bg_ref, cw_ref, cb_ref,
                    cos_ref, sin_ref, qw_ref, kw_ref,
                    qk_ref, vm_ref, mid_ref, qo_ref, ko_ref, vt_ref, g_ref, h_scr, e_scr,
                    *, chunk, q_scale_m, q_scale_a):
    t = pl.program_id(1)
    nt = pl.num_programs(1)
    tm, d = x_ref.shape
    conv_cols = qk_ref.shape[1]
    dh_m = conv_cols // (2 * MLSTM_HEADS)
    mid_cols = mid_ref.shape[1]
    qa_cols = qo_ref.shape[1]
    ka_cols = ko_ref.shape[1]
    gain = nw_ref[...] * (1.0 + mod_ref[:, d:2 * d])
    shift = mod_ref[:, 0:d]

    def normed(xv):
        ms = jnp.mean(xv * xv, axis=-1, keepdims=True)
        return (xv * lax.rsqrt(ms + EPS) * gain + shift).astype(BF16)

    h_scr[0:HALO, :] = normed(xp_ref[...])
    h_scr[HALO:HALO + tm, :] = normed(x_ref[...])
    h_scr[HALO + tm:, :] = normed(xn_ref[...])
    centre = slice(HALO, HALO + tm)
    g_ref[...] = _dot(h_scr[centre, :], wg_ref[...]) + bg_ref[...]

    half = HALO // 2
    pad = MLSTM_CONV_W // 2
    has_prev = (t > 0).astype(F32)
    has_next = (t < nt - 1).astype(F32)

    def conv_chunk(c0):
        cs = slice(c0, c0 + chunk)
        e = _dot(h_scr[...], w_ref[:, cs])
        e_scr[0:half, :] = e[half:HALO] * has_prev
        e_scr[half:half + tm, :] = e[HALO:HALO + tm]
        e_scr[half + tm:, :] = e[HALO + tm:HALO + tm + half] * has_next
        acc = None
        for j in range(MLSTM_CONV_W):
            term = e_scr[half - pad + j:half - pad + j + tm, :] * cw_ref[j:j + 1, cs]
            acc = term if acc is None else acc + term
        y = acc + cb_ref[:, cs]
        y = _silu(y)
        if c0 < conv_cols // 2:
            y = y * q_scale_m
        qk_ref[:, cs] = y.astype(qk_ref.dtype)

    vm_base = conv_cols
    vm_cols = (vm_ref.shape[0] // (dh_m + ONES_ROWS)) * dh_m

    def vm_chunk(c0):
        a = _dot(h_scr[centre, :], w_ref[:, vm_base + c0:vm_base + c0 + chunk]).astype(vm_ref.dtype)
        for hh in range(chunk // dh_m):
            r0 = (c0 // dh_m + hh) * (dh_m + ONES_ROWS)
            for c in range(0, dh_m, LANES):
                vm_ref[r0 + c:r0 + c + LANES, :] = a[:, hh * dh_m + c:hh * dh_m + c + LANES].T
            vm_ref[r0 + dh_m:r0 + dh_m + ONES_ROWS, :] = jnp.ones((ONES_ROWS, tm), vm_ref.dtype)

    mid_base = vm_base + vm_cols

    def mid_chunk(c0):
        mid_ref[:, c0:c0 + chunk] = _dot(h_scr[centre, :],
                                         w_ref[:, mid_base + c0:mid_base + c0 + chunk]).astype(mid_ref.dtype)

    qa_base = mid_base + mid_cols
    cos = cos_ref[...]
    sin = sin_ref[...]

    def q_chunk(c0):
        a = _dot(h_scr[centre, :], w_ref[:, qa_base + c0:qa_base + c0 + chunk])
        for hh in range(chunk // LANES):
            y = _norm_rope(a[:, hh * LANES:(hh + 1) * LANES], qw_ref[...], cos, sin)
            qo_ref[:, c0 + hh * LANES:c0 + (hh + 1) * LANES] = (y * q_scale_a).astype(qo_ref.dtype)

    kv_base = qa_base + qa_cols

    def kv_chunk():
        a = _dot(h_scr[centre, :], w_ref[:, kv_base:kv_base + 2 * ka_cols])
        for g in range(ka_cols // LANES):
            y = _norm_rope(a[:, g * LANES:(g + 1) * LANES], kw_ref[...], cos, sin)
            ko_ref[:, g * LANES:(g + 1) * LANES] = y.astype(ko_ref.dtype)
            r0 = g * (LANES + ONES_ROWS)
            v = a[:, ka_cols + g * LANES:ka_cols + (g + 1) * LANES].astype(vt_ref.dtype)
            vt_ref[r0:r0 + LANES, :] = v.T
            vt_ref[r0 + LANES:r0 + LANES + ONES_ROWS, :] = jnp.ones((ONES_ROWS, tm), vt_ref.dtype)

    heavy = ([functools.partial(conv_chunk, c0) for c0 in range(0, conv_cols, chunk)]
             + [functools.partial(q_chunk, c0) for c0 in range(0, qa_cols, chunk)] + [kv_chunk])
    light = ([functools.partial(vm_chunk, c0) for c0 in range(0, vm_cols, chunk)]
             + [functools.partial(mid_chunk, c0) for c0 in range(0, mid_cols, chunk)])
    for i in range(max(len(heavy), len(light))):
        for task in heavy[i:i + 1] + light[i:i + 1]:
            task()


def _in_proj(x, mod3, norm_w, w_main, w_gates, b_gates, conv_w, conv_b, cos, sin, qn_w, kn_w,
             conv_cols, mid_cols, qa_cols, ka_cols, tm, q_scale_m, q_scale_a):
    bsz, s, d = x.shape
    n = w_main.shape[1]
    vm_cols = conv_cols // 2
    assert n == conv_cols + vm_cols + mid_cols + qa_cols + 2 * ka_cols
    ng = w_gates.shape[1]
    vm_rows = MLSTM_HEADS * (vm_cols // MLSTM_HEADS + ONES_ROWS)
    vt_rows = (ka_cols // LANES) * (LANES + ONES_ROWS)
    nh = s // HALO
    chunk = 512
    kern = functools.partial(_in_proj_kernel, chunk=chunk, q_scale_m=q_scale_m, q_scale_a=q_scale_a)
    tok = lambda b, t: (b, t, 0)
    const = lambda b, t: (0, 0)
    bf = lambda cols: jax.ShapeDtypeStruct((bsz, s, cols), BF16)
    return pl.pallas_call(
        kern,
        grid=(bsz, s // tm),
        in_specs=[pl.BlockSpec((None, HALO, d), lambda b, t: (b, jnp.maximum(t * (tm // HALO) - 1, 0), 0)),
                  pl.BlockSpec((None, tm, d), tok),
                  pl.BlockSpec((None, HALO, d),
                               lambda b, t: (b, jnp.minimum((t + 1) * (tm // HALO), nh - 1), 0)),
                  pl.BlockSpec((None, 1, mod3.shape[-1]), lambda b, t: (b, 0, 0)),
                  pl.BlockSpec((1, d), const),
                  pl.BlockSpec((d, n), const, pipeline_mode=pl.Buffered(1)),
                  pl.BlockSpec((d, ng), const),
                  pl.BlockSpec((1, ng), const),
                  pl.BlockSpec((MLSTM_CONV_W, conv_cols), const),
                  pl.BlockSpec((1, conv_cols), const),
                  pl.BlockSpec((tm, LANES), lambda b, t: (t, 0)),
                  pl.BlockSpec((tm, LANES), lambda b, t: (t, 0)),
                  pl.BlockSpec((1, LANES), const),
                  pl.BlockSpec((1, LANES), const)],
        out_specs=[pl.BlockSpec((None, tm, conv_cols), tok),
                   pl.BlockSpec((None, None, vm_rows, tm), lambda b, t: (b, t, 0, 0)),
                   pl.BlockSpec((None, tm, mid_cols), tok),
                   pl.BlockSpec((None, tm, qa_cols), tok),
                   pl.BlockSpec((None, tm, ka_cols), tok),
                   pl.BlockSpec((None, None, vt_rows, tm), lambda b, t: (b, t, 0, 0)),
                   pl.BlockSpec((None, tm, ng), tok)],
        out_shape=[bf(conv_cols),
                   jax.ShapeDtypeStruct((bsz, s // tm, vm_rows, tm), BF16),
                   bf(mid_cols), bf(qa_cols), bf(ka_cols),
                   jax.ShapeDtypeStruct((bsz, s // tm, vt_rows, tm), BF16),
                   jax.ShapeDtypeStruct((bsz, s, ng), F32)],
        scratch_shapes=[pltpu.VMEM((tm + 2 * HALO, d), BF16),
                        pltpu.VMEM((tm + HALO, chunk), F32)],
        compiler_params=_params("parallel", "parallel"),
        name="in_proj",
    )(x, x, x, mod3, norm_w, w_main, w_gates, b_gates, conv_w, conv_b.reshape(1, conv_cols), cos, sin,
      qn_w.reshape(1, LANES), kn_w.reshape(1, LANES))


def _mlstm_kernel(qkf_ref, qkb_ref, vtf_ref, vtb_ref, gf_ref, gb_ref, hf_ref, hb_ref, st_scr, m_scr):
    nb, ln = qkf_ref.shape[0], qkf_ref.shape[1]
    inner = qkf_ref.shape[2] // 2
    dh = inner // MLSTM_HEADS
    vrows = dh + ONES_ROWS
    nchain = 2 * MLSTM_HEADS

    @pl.when(pl.program_id(1) == 0)
    def _():
        st_scr[...] = jnp.zeros_like(st_scr)
        m_scr[...] = jnp.full_like(m_scr, -jnp.inf)

    row = lax.broadcasted_iota(jnp.int32, (ln, ln), 0)
    col = lax.broadcasted_iota(jnp.int32, (ln, ln), 1)
    lower = col <= row
    upper = col >= row

    def gate_terms(g_ref, tri):
        gi = g_ref[:, 0:LANES]
        gf = g_ref[:, LANES:2 * LANES]
        a = jnp.minimum(gf, 0.0) - jnp.log(1.0 + jnp.exp(-jnp.abs(gf)))
        tri_b = jnp.where(tri, 1.0, 0.0).astype(BF16)
        a1 = a.astype(BF16)
        r1 = a - a1.astype(F32)
        a2 = r1.astype(BF16)
        a3 = (r1 - a2.astype(F32)).astype(BF16)
        b = _dot(tri_b, a1) + _dot(tri_b, a2) + _dot(tri_b, a3)
        return gi - b, b.T

    gate = [(gate_terms(gf_ref.at[bb], lower), gate_terms(gb_ref.at[bb], upper)) for bb in range(nb)]

    for bb, direction in [(bb, direction) for bb in range(nb) for direction in range(2)]:
        qk_ref, vt_ref, h_ref = ((qkf_ref, vtf_ref, hf_ref), (qkb_ref, vtb_ref, hb_ref))[direction]
        qk_ref, vt_ref, h_ref = qk_ref.at[bb], vt_ref.at[bb], h_ref.at[bb]
        umat, btmat = gate[bb][direction]
        mask = (upper, lower)[direction]
        for head in range(MLSTM_HEADS):
            lane_j = direction * MLSTM_HEADS + head
            j = bb * nchain + lane_j
            q = qk_ref[:, head * dh:(head + 1) * dh]
            k = qk_ref[:, inner + head * dh:inner + (head + 1) * dh]
            vt = vt_ref[head * vrows:(head + 1) * vrows, :]
            u_col = umat[:, lane_j:lane_j + 1]
            b_row = btmat[lane_j:lane_j + 1, :]
            m_prev = m_scr[j:j + 1, 0:1]
            state = st_scr[j]

            u_m = jnp.where(mask, jnp.broadcast_to(u_col, (ln, ln)), -jnp.inf)
            g_row = jnp.maximum(jnp.max(u_m, axis=0, keepdims=True), m_prev)
            w_t = _dot_nt(k, q) * jnp.exp(u_m - g_row)
            inter = jnp.exp(m_prev - g_row)
            tot = _dot(vt, w_t.astype(BF16)) + inter * _dot_nt(state.astype(BF16), q)
            floor = jnp.exp(-(g_row + b_row))
            h_t = tot[0:dh, :] / jnp.maximum(jnp.abs(tot[dh:dh + 1, :]), floor)
            h_ref[:, head * dh:(head + 1) * dh] = h_t.T

            g_end = jnp.max(g_row, axis=1, keepdims=True)
            b_tot = jnp.min(b_row, axis=1, keepdims=True)
            kw = k.astype(F32) * jnp.exp(jnp.broadcast_to(u_col, (ln, dh)) - g_end)
            st_scr[j] = jnp.exp(m_prev - g_end) * state + _dot(vt, kw.astype(BF16))
            m_scr[j:j + 1, :] = jnp.broadcast_to(b_tot + g_end, (1, m_scr.shape[1]))


def _mlstm(qk, vt, gates, ln):
    bsz, s, two_inner = qk.shape
    inner = two_inner // 2
    nc = s // ln
    ng = gates.shape[-1]
    dh = inner // MLSTM_HEADS
    nchain = 2 * MLSTM_HEADS
    vt_rows, tv = vt.shape[2], vt.shape[3]
    per = tv // ln
    nb = 2 if bsz % 2 == 0 else 1
    fwd = lambda b, i: (b, i, 0)
    bwd = lambda b, i: (b, nc - 1 - i, 0)
    return pl.pallas_call(
        _mlstm_kernel,
        grid=(bsz // nb, nc),
        in_specs=[pl.BlockSpec((nb, ln, two_inner), fwd),
                  pl.BlockSpec((nb, ln, two_inner), bwd),
                  pl.BlockSpec((nb, None, vt_rows, ln), lambda b, i: (b, i // per, 0, i % per)),
                  pl.BlockSpec((nb, None, vt_rows, ln),
                               lambda b, i: (b, (nc - 1 - i) // per, 0, (nc - 1 - i) % per)),
                  pl.BlockSpec((nb, ln, ng), fwd),
                  pl.BlockSpec((nb, ln, ng), bwd)],
        out_specs=[pl.BlockSpec((nb, ln, inner), fwd),
                   pl.BlockSpec((nb, ln, inner), bwd)],
        out_shape=[jax.ShapeDtypeStruct((bsz, s, inner), F32),
                   jax.ShapeDtypeStruct((bsz, s, inner), F32)],
        scratch_shapes=[pltpu.VMEM((nb * nchain, dh + ONES_ROWS, dh), F32),
                        pltpu.VMEM((nb * nchain, LANES), F32)],
        compiler_params=_params("parallel", "arbitrary"),
        name="mlstm",
    )(qk, qk, vt, vt, gates, gates)


def _rope_tab_kernel(freq_ref, cos_ref, sin_ref):
    n = cos_ref.shape[0]
    idx = lax.broadcasted_iota(jnp.int32, (n, LANES), 0).astype(F32)
    lane = lax.broadcasted_iota(jnp.int32, (n, LANES), 1)
    ang = idx * freq_ref[...]
    sign = jnp.where(lane < LANES // 2, -1.0, 1.0)
    cos_ref[...] = jnp.cos(ang)
    sin_ref[...] = jnp.sin(ang) * sign


def _rope_tables(s):
    n_freq = ATTN_HEAD_DIM // 4
    freqs = ROPE_THETA ** (-jnp.arange(n_freq, dtype=F32) / n_freq)
    freq_lanes = jnp.tile(freqs, 4).reshape(1, ATTN_HEAD_DIM)
    rows = s // GRID_W
    n = max(rows, GRID_W)
    assert n % 8 == 0
    cos_i, sin_i = pl.pallas_call(
        _rope_tab_kernel,
        grid=(1,),
        in_specs=[pl.BlockSpec((1, LANES), lambda t: (0, 0))],
        out_specs=[pl.BlockSpec((n, LANES), lambda t: (0, 0)),
                   pl.BlockSpec((n, LANES), lambda t: (0, 0))],
        out_shape=[jax.ShapeDtypeStruct((n, LANES), F32),
                   jax.ShapeDtypeStruct((n, LANES), F32)],
        compiler_params=_params("arbitrary"),
        name="rope_tab",
    )(freq_lanes)
    is_col = (jnp.arange(LANES) & (LANES // 4)) != 0

    def expand(tab):
        by_row = jnp.repeat(tab[:rows], GRID_W, axis=0)
        by_col = jnp.tile(tab[:GRID_W], (rows, 1))
        return jnp.where(is_col[None, :], by_col, by_row)

    return expand(cos_i), expand(sin_i)


def _attn_kernel(q_ref, k_ref, vt_ref, o_ref, qs_scr, m_scr, alpha_scr, acc_scr, st_scr, p_scr):
    tq = q_ref.shape[0]
    group = q_ref.shape[1] // LANES
    nk, tk = k_ref.shape[0], k_ref.shape[1]
    dh = LANES
    strip = 128

    for h in range(group):
        qs_scr[:, h * tq:(h + 1) * tq] = q_ref[:, h * LANES:(h + 1) * LANES].T
    m_scr[...] = jnp.full_like(m_scr, -jnp.inf)
    acc_scr[...] = jnp.zeros_like(acc_scr)

    def scores(j, slot):
        st_scr[slot] = _dot(k_ref[j], qs_scr[...]).astype(BF16)

    def softmax(slot):
        for h in range(group):
            cols = slice(h * tq, (h + 1) * tq)
            m_old = m_scr[:, cols]
            cmax = jnp.max(st_scr[slot, :, cols], axis=0, keepdims=True).astype(F32)
            m_new = jnp.maximum(m_old, cmax)
            m_b = m_new.astype(BF16)
            for r in range(0, tk, strip):
                p_scr[slot, r:r + strip, cols] = jnp.exp2(st_scr[slot, r:r + strip, cols] - m_b)
            alpha_scr[slot, :, cols] = jnp.exp2(m_old - m_new)
            m_scr[:, cols] = m_new

    def pv(j, slot):
        acc_scr[...] = alpha_scr[slot] * acc_scr[...] + _dot(vt_ref[j], p_scr[slot])

    assert nk % 2 == 0
    scores(0, 0)
    scores(1, 1)
    softmax(0)

    def body(i, carry):
        j = 2 * i + 1
        scores(j + 1, 0)
        softmax(1)
        pv(j - 1, 0)
        scores(j + 2, 1)
        softmax(0)
        pv(j, 1)
        return carry

    lax.fori_loop(0, (nk - 2) // 2, body, 0)
    softmax(1)
    pv(nk - 2, 0)
    pv(nk - 1, 1)
    for h in range(group):
        cols = slice(h * tq, (h + 1) * tq)
        out_t = acc_scr[0:dh, cols] / acc_scr[dh:dh + 1, cols]
        o_ref[:, h * LANES:(h + 1) * LANES] = out_t.T.astype(o_ref.dtype)


def _attention(q, k, vt, tq):
    bsz, s, q_inner = q.shape
    dh = ATTN_HEAD_DIM
    nk, tk = vt.shape[1], vt.shape[3]
    kvh = k.shape[-1] // dh
    gw = q_inner // kvh
    group = gw // dh
    vrows = dh + ONES_ROWS
    k4 = k.reshape(bsz, nk, tk, kvh * dh)
    return pl.pallas_call(
        _attn_kernel,
        grid=(bsz, kvh, s // tq),
        in_specs=[pl.BlockSpec((None, tq, gw), lambda b, g, i: (b, i, g)),
                  pl.BlockSpec((None, nk, tk, dh), lambda b, g, i: (b, 0, 0, g)),
                  pl.BlockSpec((None, nk, vrows, tk), lambda b, g, i: (b, 0, g, 0))],
        out_specs=pl.BlockSpec((None, tq, gw), lambda b, g, i: (b, i, g)),
        out_shape=jax.ShapeDtypeStruct((bsz, s, q_inner), BF16),
        scratch_shapes=[pltpu.VMEM((dh, group * tq), BF16),
                        pltpu.VMEM((1, group * tq), F32),
                        pltpu.VMEM((2, 1, group * tq), F32),
                        pltpu.VMEM((vrows, group * tq), F32),
                        pltpu.VMEM((2, tk, group * tq), BF16),
                        pltpu.VMEM((2, tk, group * tq), BF16)],
        compiler_params=_params("parallel", "parallel", "parallel"),
        name="attn",
    )(q, k4, vt)


def _post_kernel(hf_ref, hb_ref, o_ref, bm_ref, ba_ref, ya_ref, x_ref, mod_ref, gn_ref, n1_ref, wm_ref,
                 wa_ref, wo_ref, n2a_ref, n2b_ref, w1_ref, w2_ref, out_ref, ym_scr, x1_scr, *, ff_chunk):
    d = x_ref.shape[-1]
    dh = d // MLSTM_HEADS
    for head in range(MLSTM_HEADS):
        sl = slice(head * dh, (head + 1) * dh)
        h = hf_ref[:, sl] + hb_ref[:, sl]
        mu = jnp.mean(h, axis=-1, keepdims=True)
        hc = h - mu
        var = jnp.mean(hc * hc, axis=-1, keepdims=True)
        hn = hc * lax.rsqrt(var + EPS) * gn_ref[:, sl]
        ym_scr[:, sl] = (hn * _sigmoid(o_ref[:, sl].astype(F32))).astype(BF16)
    pm = _dot(ym_scr[...], wm_ref[...])
    pa = _dot(ya_ref[...], wa_ref[...])
    y = _sigmoid(bm_ref[...].astype(F32)) * pm + _sigmoid(ba_ref[...].astype(F32)) * pa
    y2 = _dot(y.astype(BF16), wo_ref[...])
    ms = jnp.mean(y2 * y2, axis=-1, keepdims=True)
    x1_scr[...] = x_ref[...] + mod_ref[:, 2 * d:3 * d] * (y2 * lax.rsqrt(ms + EPS) * n1_ref[...])

    x1 = x1_scr[...]
    ms1 = jnp.mean(x1 * x1, axis=-1, keepdims=True)
    shift = mod_ref[:, 3 * d:4 * d]
    scale = mod_ref[:, 4 * d:5 * d]
    gate = mod_ref[:, 5 * d:6 * d]
    h2 = (x1 * lax.rsqrt(ms1 + EPS) * n2a_ref[...] * (1.0 + scale) + shift).astype(BF16)
    acc = None
    for c in range(w1_ref.shape[1] // ff_chunk):
        sl = slice(c * ff_chunk, (c + 1) * ff_chunk)
        u = jnp.maximum(_dot(h2, w1_ref[:, sl]), 0.0)
        part = _dot((u * u).astype(BF16), w2_ref[sl, :])
        acc = part if acc is None else acc + part
    ms2 = jnp.mean(acc * acc, axis=-1, keepdims=True)
    out_ref[...] = x1_scr[...] + gate * (acc * lax.rsqrt(ms2 + EPS) * n2b_ref[...])


def _post(hf, hb, mid, ya, x, mod3, gn_w, n1_post, wm, wa, wo, n2_pre, n2_post, w1, w2,
          o_block, bm_block, ba_block, tm, ff_chunk):
    bsz, s, d = x.shape
    ff = w1.shape[1]
    tok = lambda b, t: (b, t, 0)
    const = lambda b, t: (0, 0)
    resident = lambda shape: pl.BlockSpec(shape, const, pipeline_mode=pl.Buffered(1))
    kern = functools.partial(_post_kernel, ff_chunk=ff_chunk)
    return pl.pallas_call(
        kern,
        grid=(bsz, s // tm),
        in_specs=[pl.BlockSpec((None, tm, d), tok),
                  pl.BlockSpec((None, tm, d), tok),
                  pl.BlockSpec((None, tm, d), lambda b, t: (b, t, o_block)),
                  pl.BlockSpec((None, tm, d), lambda b, t: (b, t, bm_block)),
                  pl.BlockSpec((None, tm, d), lambda b, t: (b, t, ba_block)),
                  pl.BlockSpec((None, tm, d), tok),
                  pl.BlockSpec((None, tm, d), tok),
                  pl.BlockSpec((None, 1, mod3.shape[-1]), lambda b, t: (b, 0, 0)),
                  pl.BlockSpec((1, d), const),
                  pl.BlockSpec((1, d), const),
                  resident((d, d)), resident((d, d)), resident((d, d)),
                  pl.BlockSpec((1, d), const),
                  pl.BlockSpec((1, d), const),
                  resident((d, ff)), resident((ff, d))],
        out_specs=pl.BlockSpec((None, tm, d), tok),
        out_shape=jax.ShapeDtypeStruct((bsz, s, d), F32),
        scratch_shapes=[pltpu.VMEM((tm, d), BF16), pltpu.VMEM((tm, d), F32)],
        compiler_params=_params("parallel", "parallel"),
        name="post",
    )(hf, hb, mid, mid, mid, ya, x, mod3, gn_w, n1_post, wm, wa, wo, n2_pre, n2_post, w1, w2)


def _layer(x, mod, norm1_pre, norm1_post, w_in, b_gates, conv_w, conv_b, mlstm_gn, attn_qnorm,
           attn_knorm, w_branch_m, w_branch_a, w_out, norm2_pre, norm2_post, w_mlp_in, w_mlp_out):
    bsz, s, d = x.shape
    inner = d
    n_gate = 4 * MLSTM_HEADS
    q_inner = d
    kv_inner = ATTN_KV_HEADS * ATTN_HEAD_DIM

    o_g = 4 * inner
    o_qa = o_g + n_gate
    o_ka = o_qa + q_inner
    o_va = o_ka + kv_inner
    o_br = o_va + kv_inner
    perm = _rope_head_perm()
    head_perm = lambda w: w.reshape(d, -1, ATTN_HEAD_DIM)[:, :, perm].reshape(d, -1)
    w_parts = [w_in[:, 0:o_g], w_in[:, o_br:o_br + 2 * d], head_perm(w_in[:, o_qa:o_va]), w_in[:, o_va:o_br]]
    w_main = jnp.concatenate([part.astype(BF16) for part in w_parts], axis=1)
    conv_cols, mid_cols = 2 * inner, inner + 2 * d
    hm = MLSTM_HEADS
    wg = w_in[:, o_g:o_qa]
    gate_cols = lambda a: (jnp.concatenate([a[..., 0:hm], a[..., 2 * hm:3 * hm]], -1),
                           jnp.concatenate([a[..., hm:2 * hm], a[..., 3 * hm:4 * hm]], -1))
    wi, wf = gate_cols(wg)
    bi, bf = gate_cols(b_gates)
    lane_pad = lambda a: jnp.pad(a, [(0, 0)] * (a.ndim - 1) + [(0, LANES - a.shape[-1])])
    w_gates = jnp.concatenate([lane_pad(wi), lane_pad(wf)], -1).astype(BF16)
    bias_gates = jnp.concatenate([lane_pad(bi), lane_pad(bf)], -1).reshape(1, 2 * LANES)

    mod3 = mod.reshape(bsz, 1, mod.shape[-1])
    cos, sin = _rope_tables(s)
    dh_m = inner // MLSTM_HEADS
    qk, vt_m, mid, q_rot, k_rot, v_t, gates = _in_proj(
        x, mod3, norm1_pre.reshape(1, d), w_main, w_gates, bias_gates, conv_w, conv_b, cos, sin,
        attn_qnorm[perm], attn_knorm[perm], conv_cols=conv_cols, mid_cols=mid_cols, qa_cols=q_inner,
        ka_cols=kv_inner, tm=min(s, 512), q_scale_m=dh_m ** -0.5,
        q_scale_a=ATTN_HEAD_DIM ** -0.5 * math.log2(math.e))

    h_f, h_b = _mlstm(qk, vt_m, gates, ln=min(s, 256))
    y_a = _attention(q_rot, k_rot, v_t, tq=min(s, 1024))

    return _post(h_f, h_b, mid, y_a, x, mod3, mlstm_gn.reshape(1, d), norm1_post.reshape(1, d),
                 w_branch_m.astype(BF16), w_branch_a.astype(BF16), w_out.astype(BF16),
                 norm2_pre.reshape(1, d), norm2_post.reshape(1, d), w_mlp_in.astype(BF16),
                 w_mlp_out.astype(BF16), o_block=0, bm_block=1, ba_block=2, tm=min(s, 512), ff_chunk=1024)


def kernel(x, c, w_ada, b_ada, norm1_pre, norm1_post, w_in, b_gates, conv_w, conv_b, mlstm_gn, attn_qnorm, attn_knorm, w_branch_m, w_branch_a, w_out, norm2_pre, norm2_post, w_mlp_in, w_mlp_out):
    for l in range(w_ada.shape[0]):
        mod = _ada(c, w_ada[l], b_ada[l])
        x = _layer(x, mod, norm1_pre[l], norm1_post[l], w_in[l], b_gates[l], conv_w[l], conv_b[l],
                   mlstm_gn[l], attn_qnorm[l], attn_knorm[l], w_branch_m[l], w_branch_a[l], w_out[l],
                   norm2_pre[l], norm2_post[l], w_mlp_in[l], w_mlp_out[l])
    return x
```

```python
import functools
import math

import jax
import jax.numpy as jnp
from jax import lax
from jax.experimental import pallas as pl
from jax.experimental.pallas import tpu as pltpu

EPS = 1e-6
GRID_W = 64
ROPE_THETA = 10000.0
MLSTM_HEADS = 4
MLSTM_CONV_W = 5
ATTN_HEAD_DIM = 128
ATTN_KV_HEADS = 2
LANES = 128
HALO = 16
ONES_ROWS = 16

F32 = jnp.float32
BF16 = jnp.bfloat16
VMEM_LIMIT = 56 * 1024 * 1024


def _params(*sem):
    return pltpu.CompilerParams(dimension_semantics=sem, vmem_limit_bytes=VMEM_LIMIT)


def _sigmoid(x):
    return 0.5 * jnp.tanh(0.5 * x) + 0.5


def _silu(x):
    half = 0.5 * x
    return half * jnp.tanh(half) + half


def _dot(a, b):
    return jnp.dot(a, b, preferred_element_type=F32)


def _dot_nt(a, b):
    return lax.dot_general(a, b, (((1,), (1,)), ((), ())), preferred_element_type=F32)


def _ada_kernel(c_ref, w_ref, b_ref, o_ref):
    c = c_ref[...]
    sc = _silu(c)
    o_ref[...] = jnp.dot(sc, w_ref[...], preferred_element_type=F32,
                         precision=lax.Precision.HIGHEST) + b_ref[...]


def _ada(c, w, b):
    bsz, d = c.shape
    n = w.shape[1]
    tn = 1024
    return pl.pallas_call(
        _ada_kernel,
        grid=(n // tn,),
        in_specs=[pl.BlockSpec((bsz, d), lambda j: (0, 0)),
                  pl.BlockSpec((d, tn), lambda j: (0, j)),
                  pl.BlockSpec((1, tn), lambda j: (0, j))],
        out_specs=pl.BlockSpec((bsz, tn), lambda j: (0, j)),
        out_shape=jax.ShapeDtypeStruct((bsz, n), F32),
        compiler_params=_params("parallel"),
        name="ada",
    )(c, w, b.reshape(1, n))


def _rope_head_perm():
    quarter = ATTN_HEAD_DIM // 4
    blocks = [0, 2, 1, 3]
    return jnp.concatenate([jnp.arange(b * quarter, (b + 1) * quarter) for b in blocks])


def _norm_rope(x, w, cos, sin):
    ms = jnp.mean(x * x, axis=-1, keepdims=True)
    y = x * lax.rsqrt(ms + EPS) * w
    return y * cos + pltpu.roll(y, LANES // 2, 1) * sin


def _in_proj_kernel(xp_ref, x_ref, xn_ref, mod_ref, nw_ref, w_ref, wg_ref, bg_ref, cw_ref, cb_ref,
                    cos_ref, sin_ref, qw_ref, kw_ref,
                    qk_ref, vm_ref, mid_ref, qo_ref, ko_ref, vt_ref, g_ref, h_scr, e_scr,
                    *, chunk, q_scale_m, q_scale_a):
    t = pl.program_id(1)
    nt = pl.num_programs(1)
    tm, d = x_ref.shape
    conv_cols = qk_ref.shape[1]
    dh_m = conv_cols // (2 * MLSTM_HEADS)
    mid_cols = mid_ref.shape[1]
    qa_cols = qo_ref.shape[1]
    ka_cols = ko_ref.shape[1]
    gain = nw_ref[...] * (1.0 + mod_ref[:, d:2 * d])
    shift = mod_ref[:, 0:d]

    def normed(xv):
        ms = jnp.mean(xv * xv, axis=-1, keepdims=True)
        return (xv * lax.rsqrt(ms + EPS) * gain + shift).astype(BF16)

    h_scr[0:HALO, :] = normed(xp_ref[...])
    h_scr[HALO:HALO + tm, :] = normed(x_ref[...])
    h_scr[HALO + tm:, :] = normed(xn_ref[...])
    centre = slice(HALO, HALO + tm)
    g_ref[...] = _dot(h_scr[centre, :], wg_ref[...]) + bg_ref[...]

    half = HALO // 2
    pad = MLSTM_CONV_W // 2
    has_prev = (t > 0).astype(F32)
    has_next = (t < nt - 1).astype(F32)

    def conv_chunk(c0):
        cs = slice(c0, c0 + chunk)
        e = _dot(h_scr[...], w_ref[:, cs])
        e_scr[0:half, :] = e[half:HALO] * has_prev
        e_scr[half:half + tm, :] = e[HALO:HALO + tm]
        e_scr[half + tm:, :] = e[HALO + tm:HALO + tm + half] * has_next
        acc = None
        for j in range(MLSTM_CONV_W):
            term = e_scr[half - pad + j:half - pad + j + tm, :] * cw_ref[j:j + 1, cs]
            acc = term if acc is None else acc + term
        y = acc + cb_ref[:, cs]
        y = _silu(y)
        if c0 < conv_cols // 2:
            y = y * q_scale_m
        qk_ref[:, cs] = y.astype(qk_ref.dtype)

    vm_base = conv_cols
    vm_cols = (vm_ref.shape[0] // (dh_m + ONES_ROWS)) * dh_m

    def vm_chunk(c0):
        a = _dot(h_scr[centre, :], w_ref[:, vm_base + c0:vm_base + c0 + chunk]).astype(vm_ref.dtype)
        for hh in range(chunk // dh_m):
            r0 = (c0 // dh_m + hh) * (dh_m + ONES_ROWS)
            for c in range(0, dh_m, LANES):
                vm_ref[r0 + c:r0 + c + LANES, :] = a[:, hh * dh_m + c:hh * dh_m + c + LANES].T
            vm_ref[r0 + dh_m:r0 + dh_m + ONES_ROWS, :] = jnp.ones((ONES_ROWS, tm), vm_ref.dtype)

    mid_base = vm_base + vm_cols

    def mid_chunk(c0):
        mid_ref[:, c0:c0 + chunk] = _dot(h_scr[centre, :],
                                         w_ref[:, mid_base + c0:mid_base + c0 + chunk]).astype(mid_ref.dtype)

    qa_base = mid_base + mid_cols
    cos = cos_ref[...]
    sin = sin_ref[...]

    def q_chunk(c0):
        a = _dot(h_scr[centre, :], w_ref[:, qa_base + c0:qa_base + c0 + chunk])
        for hh in range(chunk // LANES):
            y = _norm_rope(a[:, hh * LANES:(hh + 1) * LANES], qw_ref[...], cos, sin)
            qo_ref[:, c0 + hh * LANES:c0 + (hh + 1) * LANES] = (y * q_scale_a).astype(qo_ref.dtype)

    kv_base = qa_base + qa_cols

    def kv_chunk():
        a = _dot(h_scr[centre, :], w_ref[:, kv_base:kv_base + 2 * ka_cols])
        for g in range(ka_cols // LANES):
            y = _norm_rope(a[:, g * LANES:(g + 1) * LANES], kw_ref[...], cos, sin)
            ko_ref[:, g * LANES:(g + 1) * LANES] = y.astype(ko_ref.dtype)
            r0 = g * (LANES + ONES_ROWS)
            v = a[:, ka_cols + g * LANES:ka_cols + (g + 1) * LANES].astype(vt_ref.dtype)
            vt_ref[r0:r0 + LANES, :] = v.T
            vt_ref[r0 + LANES:r0 + LANES + ONES_ROWS, :] = jnp.ones((ONES_ROWS, tm), vt_ref.dtype)

    convs = [functools.partial(conv_chunk, c0) for c0 in range(0, conv_cols, chunk)]
    rots = [functools.partial(q_chunk, c0) for c0 in range(0, qa_cols, chunk)] + [kv_chunk]
    light = ([functools.partial(vm_chunk, c0) for c0 in range(0, vm_cols, chunk)]
             + [functools.partial(mid_chunk, c0) for c0 in range(0, mid_cols, chunk)])
    n = len(convs)
    for i in range(n):
        for task in convs[i:i + 1] + light[2 * i:2 * i + 1] + rots[i:i + 1] + light[2 * i + 1:2 * i + 2]:
            task()
    for task in rots[n:] + light[2 * n:]:
        task()


def _in_proj(x, mod3, norm_w, w_main, w_gates, b_gates, conv_w, conv_b, cos, sin, qn_w, kn_w,
             conv_cols, mid_cols, qa_cols, ka_cols, tm, q_scale_m, q_scale_a):
    bsz, s, d = x.shape
    n = w_main.shape[1]
    vm_cols = conv_cols // 2
    assert n == conv_cols + vm_cols + mid_cols + qa_cols + 2 * ka_cols
    ng = w_gates.shape[1]
    vm_rows = MLSTM_HEADS * (vm_cols // MLSTM_HEADS + ONES_ROWS)
    vt_rows = (ka_cols // LANES) * (LANES + ONES_ROWS)
    nh = s // HALO
    chunk = 512
    kern = functools.partial(_in_proj_kernel, chunk=chunk, q_scale_m=q_scale_m, q_scale_a=q_scale_a)
    tok = lambda b, t: (b, t, 0)
    const = lambda b, t: (0, 0)
    bf = lambda cols: jax.ShapeDtypeStruct((bsz, s, cols), BF16)
    return pl.pallas_call(
        kern,
        grid=(bsz, s // tm),
        in_specs=[pl.BlockSpec((None, HALO, d), lambda b, t: (b, jnp.maximum(t * (tm // HALO) - 1, 0), 0)),
                  pl.BlockSpec((None, tm, d), tok),
                  pl.BlockSpec((None, HALO, d),
                               lambda b, t: (b, jnp.minimum((t + 1) * (tm // HALO), nh - 1), 0)),
                  pl.BlockSpec((None, 1, mod3.shape[-1]), lambda b, t: (b, 0, 0)),
                  pl.BlockSpec((1, d), const),
                  pl.BlockSpec((d, n), const, pipeline_mode=pl.Buffered(1)),
                  pl.BlockSpec((d, ng), const),
                  pl.BlockSpec((1, ng), const),
                  pl.BlockSpec((MLSTM_CONV_W, conv_cols), const),
                  pl.BlockSpec((1, conv_cols), const),
                  pl.BlockSpec((tm, LANES), lambda b, t: (t, 0)),
                  pl.BlockSpec((tm, LANES), lambda b, t: (t, 0)),
                  pl.BlockSpec((1, LANES), const),
                  pl.BlockSpec((1, LANES), const)],
        out_specs=[pl.BlockSpec((None, tm, conv_cols), tok),
                   pl.BlockSpec((None, None, vm_rows, tm), lambda b, t: (b, t, 0, 0)),
                   pl.BlockSpec((None, tm, mid_cols), tok),
                   pl.BlockSpec((None, tm, qa_cols), tok),
                   pl.BlockSpec((None, tm, ka_cols), tok),
                   pl.BlockSpec((None, None, vt_rows, tm), lambda b, t: (b, t, 0, 0)),
                   pl.BlockSpec((None, tm, ng), tok)],
        out_shape=[bf(conv_cols),
                   jax.ShapeDtypeStruct((bsz, s // tm, vm_rows, tm), BF16),
                   bf(mid_cols), bf(qa_cols), bf(ka_cols),
                   jax.ShapeDtypeStruct((bsz, s // tm, vt_rows, tm), BF16),
                   jax.ShapeDtypeStruct((bsz, s, ng), F32)],
        scratch_shapes=[pltpu.VMEM((tm + 2 * HALO, d), BF16),
                        pltpu.VMEM((tm + HALO, chunk), F32)],
        compiler_params=_params("parallel", "parallel"),
        name="in_proj",
    )(x, x, x, mod3, norm_w, w_main, w_gates, b_gates, conv_w, conv_b.reshape(1, conv_cols), cos, sin,
      qn_w.reshape(1, LANES), kn_w.reshape(1, LANES))


def _mlstm_kernel(qkf_ref, qkb_ref, vtf_ref, vtb_ref, gf_ref, gb_ref, hf_ref, hb_ref, st_scr, m_scr):
    nb, ln = qkf_ref.shape[0], qkf_ref.shape[1]
    inner = qkf_ref.shape[2] // 2
    dh = inner // MLSTM_HEADS
    vrows = dh + ONES_ROWS
    nchain = 2 * MLSTM_HEADS

    @pl.when(pl.program_id(1) == 0)
    def _():
        st_scr[...] = jnp.zeros_like(st_scr)
        m_scr[...] = jnp.full_like(m_scr, -jnp.inf)

    row = lax.broadcasted_iota(jnp.int32, (ln, ln), 0)
    col = lax.broadcasted_iota(jnp.int32, (ln, ln), 1)
    lower = col <= row
    upper = col >= row

    def gate_terms(g_ref, tri):
        gi = g_ref[:, 0:LANES]
        gf = g_ref[:, LANES:2 * LANES]
        a = jnp.minimum(gf, 0.0) - jnp.log(1.0 + jnp.exp(-jnp.abs(gf)))
        tri_b = jnp.where(tri, 1.0, 0.0).astype(BF16)
        a1 = a.astype(BF16)
        r1 = a - a1.astype(F32)
        a2 = r1.astype(BF16)
        a3 = (r1 - a2.astype(F32)).astype(BF16)
        b = _dot(tri_b, a1) + _dot(tri_b, a2) + _dot(tri_b, a3)
        return gi - b, b.T

    gate = [(gate_terms(gf_ref.at[bb], lower), gate_terms(gb_ref.at[bb], upper)) for bb in range(nb)]

    for bb, direction in [(bb, direction) for bb in range(nb) for direction in range(2)]:
        qk_ref, vt_ref, h_ref = ((qkf_ref, vtf_ref, hf_ref), (qkb_ref, vtb_ref, hb_ref))[direction]
        qk_ref, vt_ref, h_ref = qk_ref.at[bb], vt_ref.at[bb], h_ref.at[bb]
        umat, btmat = gate[bb][direction]
        mask = (upper, lower)[direction]
        for head in range(MLSTM_HEADS):
            lane_j = direction * MLSTM_HEADS + head
            j = bb * nchain + lane_j
            q = qk_ref[:, head * dh:(head + 1) * dh]
            k = qk_ref[:, inner + head * dh:inner + (head + 1) * dh]
            vt = vt_ref[head * vrows:(head + 1) * vrows, :]
            u_col = umat[:, lane_j:lane_j + 1]
            b_row = btmat[lane_j:lane_j + 1, :]
            m_prev = m_scr[j:j + 1, 0:1]
            state = st_scr[j]

            u_m = jnp.where(mask, jnp.broadcast_to(u_col, (ln, ln)), -jnp.inf)
            g_row = jnp.maximum(jnp.max(u_m, axis=0, keepdims=True), m_prev)
            w_t = _dot_nt(k, q) * jnp.exp(u_m - g_row)
            inter = jnp.exp(m_prev - g_row)
            tot = _dot(vt, w_t.astype(BF16)) + inter * _dot_nt(state.astype(BF16), q)
            floor = jnp.exp(-(g_row + b_row))
            h_t = tot[0:dh, :] / jnp.maximum(jnp.abs(tot[dh:dh + 1, :]), floor)
            h_ref[:, head * dh:(head + 1) * dh] = h_t.T

            g_end = jnp.max(g_row, axis=1, keepdims=True)
            b_tot = jnp.min(b_row, axis=1, keepdims=True)
            kw = k.astype(F32) * jnp.exp(jnp.broadcast_to(u_col, (ln, dh)) - g_end)
            st_scr[j] = jnp.exp(m_prev - g_end) * state + _dot(vt, kw.astype(BF16))
            m_scr[j:j + 1, :] = jnp.broadcast_to(b_tot + g_end, (1, m_scr.shape[1]))


def _mlstm(qk, vt, gates, ln):
    bsz, s, two_inner = qk.shape
    inner = two_inner // 2
    nc = s // ln
    ng = gates.shape[-1]
    dh = inner // MLSTM_HEADS
    nchain = 2 * MLSTM_HEADS
    vt_rows, tv = vt.shape[2], vt.shape[3]
    per = tv // ln
    nb = 2 if bsz % 2 == 0 else 1
    fwd = lambda b, i: (b, i, 0)
    bwd = lambda b, i: (b, nc - 1 - i, 0)
    return pl.pallas_call(
        _mlstm_kernel,
        grid=(bsz // nb, nc),
        in_specs=[pl.BlockSpec((nb, ln, two_inner), fwd),
                  pl.BlockSpec((nb, ln, two_inner), bwd),
                  pl.BlockSpec((nb, None, vt_rows, ln), lambda b, i: (b, i // per, 0, i % per)),
                  pl.BlockSpec((nb, None, vt_rows, ln),
                               lambda b, i: (b, (nc - 1 - i) // per, 0, (nc - 1 - i) % per)),
                  pl.BlockSpec((nb, ln, ng), fwd),
                  pl.BlockSpec((nb, ln, ng), bwd)],
        out_specs=[pl.BlockSpec((nb, ln, inner), fwd),
                   pl.BlockSpec((nb, ln, inner), bwd)],
        out_shape=[jax.ShapeDtypeStruct((bsz, s, inner), F32),
                   jax.ShapeDtypeStruct((bsz, s, inner), F32)],
        scratch_shapes=[pltpu.VMEM((nb * nchain, dh + ONES_ROWS, dh), F32),
                        pltpu.VMEM((nb * nchain, LANES), F32)],
        compiler_params=_params("parallel", "arbitrary"),
        name="mlstm",
    )(qk, qk, vt, vt, gates, gates)


def _rope_tab_kernel(freq_ref, cos_ref, sin_ref):
    n = cos_ref.shape[0]
    idx = lax.broadcasted_iota(jnp.int32, (n, LANES), 0).astype(F32)
    lane = lax.broadcasted_iota(jnp.int32, (n, LANES), 1)
    ang = idx * freq_ref[...]
    sign = jnp.where(lane < LANES // 2, -1.0, 1.0)
    cos_ref[...] = jnp.cos(ang)
    sin_ref[...] = jnp.sin(ang) * sign


def _rope_tables(s):
    n_freq = ATTN_HEAD_DIM // 4
    freqs = ROPE_THETA ** (-jnp.arange(n_freq, dtype=F32) / n_freq)
    freq_lanes = jnp.tile(freqs, 4).reshape(1, ATTN_HEAD_DIM)
    rows = s // GRID_W
    n = max(rows, GRID_W)
    assert n % 8 == 0
    cos_i, sin_i = pl.pallas_call(
        _rope_tab_kernel,
        grid=(1,),
        in_specs=[pl.BlockSpec((1, LANES), lambda t: (0, 0))],
        out_specs=[pl.BlockSpec((n, LANES), lambda t: (0, 0)),
                   pl.BlockSpec((n, LANES), lambda t: (0, 0))],
        out_shape=[jax.ShapeDtypeStruct((n, LANES), F32),
                   jax.ShapeDtypeStruct((n, LANES), F32)],
        compiler_params=_params("arbitrary"),
        name="rope_tab",
    )(freq_lanes)
    is_col = (jnp.arange(LANES) & (LANES // 4)) != 0

    def expand(tab):
        by_row = jnp.repeat(tab[:rows], GRID_W, axis=0)
        by_col = jnp.tile(tab[:GRID_W], (rows, 1))
        return jnp.where(is_col[None, :], by_col, by_row)

    return expand(cos_i), expand(sin_i)


def _attn_kernel(q_ref, k_ref, vt_ref, o_ref, qs_scr, m_scr, alpha_scr, acc_scr, st_scr, p_scr):
    tq = q_ref.shape[0]
    group = q_ref.shape[1] // LANES
    nk, tk = k_ref.shape[0], k_ref.shape[1]
    dh = LANES
    strip = 128

    for h in range(group):
        qs_scr[:, h * tq:(h + 1) * tq] = q_ref[:, h * LANES:(h + 1) * LANES].T
    m_scr[...] = jnp.full_like(m_scr, -jnp.inf)
    acc_scr[...] = jnp.zeros_like(acc_scr)

    def scores(j, slot):
        st_scr[slot] = _dot(k_ref[j], qs_scr[...]).astype(BF16)

    def softmax(slot):
        for h in range(group):
            cols = slice(h * tq, (h + 1) * tq)
            m_old = m_scr[:, cols]
            cmax = jnp.max(st_scr[slot, :, cols], axis=0, keepdims=True).astype(F32)
            m_new = jnp.maximum(m_old, cmax)
            m_b = m_new.astype(BF16)
            for r in range(0, tk, strip):
                p_scr[slot, r:r + strip, cols] = jnp.exp2(st_scr[slot, r:r + strip, cols] - m_b)
            alpha_scr[slot, :, cols] = jnp.exp2(m_old - m_new)
            m_scr[:, cols] = m_new

    def pv(j, slot):
        acc_scr[...] = alpha_scr[slot] * acc_scr[...] + _dot(vt_ref[j], p_scr[slot])

    assert nk % 2 == 0
    scores(0, 0)
    scores(1, 1)
    softmax(0)

    def body(i, carry):
        j = 2 * i + 1
        scores(j + 1, 0)
        softmax(1)
        pv(j - 1, 0)
        scores(j + 2, 1)
        softmax(0)
        pv(j, 1)
        return carry

    lax.fori_loop(0, (nk - 2) // 2, body, 0)
    softmax(1)
    pv(nk - 2, 0)
    pv(nk - 1, 1)
    for h in range(group):
        cols = slice(h * tq, (h + 1) * tq)
        out_t = acc_scr[0:dh, cols] / acc_scr[dh:dh + 1, cols]
        o_ref[:, h * LANES:(h + 1) * LANES] = out_t.T.astype(o_ref.dtype)


def _attention(q, k, vt, tq):
    bsz, s, q_inner = q.shape
    dh = ATTN_HEAD_DIM
    nk, tk = vt.shape[1], vt.shape[3]
    kvh = k.shape[-1] // dh
    gw = q_inner // kvh
    group = gw // dh
    vrows = dh + ONES_ROWS
    k4 = k.reshape(bsz, nk, tk, kvh * dh)
    return pl.pallas_call(
        _attn_kernel,
        grid=(bsz, kvh, s // tq),
        in_specs=[pl.BlockSpec((None, tq, gw), lambda b, g, i: (b, i, g)),
                  pl.BlockSpec((None, nk, tk, dh), lambda b, g, i: (b, 0, 0, g)),
                  pl.BlockSpec((None, nk, vrows, tk), lambda b, g, i: (b, 0, g, 0))],
        out_specs=pl.BlockSpec((None, tq, gw), lambda b, g, i: (b, i, g)),
        out_shape=jax.ShapeDtypeStruct((bsz, s, q_inner), BF16),
        scratch_shapes=[pltpu.VMEM((dh, group * tq), BF16),
                        pltpu.VMEM((1, group * tq), F32),
                        pltpu.VMEM((2, 1, group * tq), F32),
                        pltpu.VMEM((vrows, group * tq), F32),
                        pltpu.VMEM((2, tk, group * tq), BF16),
                        pltpu.VMEM((2, tk, group * tq), BF16)],
        compiler_params=_params("parallel", "parallel", "parallel"),
        name="attn",
    )(q, k4, vt)


def _post_kernel(hf_ref, hb_ref, o_ref, bm_ref, ba_ref, ya_ref, x_ref, mod_ref, gn_ref, n1_ref, wm_ref,
                 wa_ref, wo_ref, n2a_ref, n2b_ref, w1_ref, w2_ref, out_ref, ym_scr, x1_scr, *, ff_chunk):
    d = x_ref.shape[-1]
    dh = d // MLSTM_HEADS
    pa = _dot(ya_ref[...], wa_ref[...])
    pm = None
    for head in range(MLSTM_HEADS):
        sl = slice(head * dh, (head + 1) * dh)
        h = hf_ref[:, sl] + hb_ref[:, sl]
        mu = jnp.mean(h, axis=-1, keepdims=True)
        hc = h - mu
        var = jnp.mean(hc * hc, axis=-1, keepdims=True)
        hn = hc * lax.rsqrt(var + EPS) * gn_ref[:, sl]
        ym_scr[:, sl] = (hn * _sigmoid(o_ref[:, sl].astype(F32))).astype(BF16)
        part = _dot(ym_scr[:, sl], wm_ref[sl, :])
        pm = part if pm is None else pm + part
    y = _sigmoid(bm_ref[...].astype(F32)) * pm + _sigmoid(ba_ref[...].astype(F32)) * pa
    y2 = _dot(y.astype(BF16), wo_ref[...])
    ms = jnp.mean(y2 * y2, axis=-1, keepdims=True)
    x1_scr[...] = x_ref[...] + mod_ref[:, 2 * d:3 * d] * (y2 * lax.rsqrt(ms + EPS) * n1_ref[...])

    x1 = x1_scr[...]
    ms1 = jnp.mean(x1 * x1, axis=-1, keepdims=True)
    shift = mod_ref[:, 3 * d:4 * d]
    scale = mod_ref[:, 4 * d:5 * d]
    gate = mod_ref[:, 5 * d:6 * d]
    h2 = (x1 * lax.rsqrt(ms1 + EPS) * n2a_ref[...] * (1.0 + scale) + shift).astype(BF16)
    acc = None
    for c in range(w1_ref.shape[1] // ff_chunk):
        sl = slice(c * ff_chunk, (c + 1) * ff_chunk)
        u = jnp.maximum(_dot(h2, w1_ref[:, sl]), 0.0)
        part = _dot((u * u).astype(BF16), w2_ref[sl, :])
        acc = part if acc is None else acc + part
    ms2 = jnp.mean(acc * acc, axis=-1, keepdims=True)
    out_ref[...] = x1_scr[...] + gate * (acc * lax.rsqrt(ms2 + EPS) * n2b_ref[...])


def _post(hf, hb, mid, ya, x, mod3, gn_w, n1_post, wm, wa, wo, n2_pre, n2_post, w1, w2,
          o_block, bm_block, ba_block, tm, ff_chunk):
    bsz, s, d = x.shape
    ff = w1.shape[1]
    tok = lambda b, t: (b, t, 0)
    const = lambda b, t: (0, 0)
    resident = lambda shape: pl.BlockSpec(shape, const, pipeline_mode=pl.Buffered(1))
    kern = functools.partial(_post_kernel, ff_chunk=ff_chunk)
    return pl.pallas_call(
        kern,
        grid=(bsz, s // tm),
        in_specs=[pl.BlockSpec((None, tm, d), tok),
                  pl.BlockSpec((None, tm, d), tok),
                  pl.BlockSpec((None, tm, d), lambda b, t: (b, t, o_block)),
                  pl.BlockSpec((None, tm, d), lambda b, t: (b, t, bm_block)),
                  pl.BlockSpec((None, tm, d), lambda b, t: (b, t, ba_block)),
                  pl.BlockSpec((None, tm, d), tok),
                  pl.BlockSpec((None, tm, d), tok),
                  pl.BlockSpec((None, 1, mod3.shape[-1]), lambda b, t: (b, 0, 0)),
                  pl.BlockSpec((1, d), const),
                  pl.BlockSpec((1, d), const),
                  resident((d, d)), resident((d, d)), resident((d, d)),
                  pl.BlockSpec((1, d), const),
                  pl.BlockSpec((1, d), const),
                  resident((d, ff)), resident((ff, d))],
        out_specs=pl.BlockSpec((None, tm, d), tok),
        out_shape=jax.ShapeDtypeStruct((bsz, s, d), F32),
        scratch_shapes=[pltpu.VMEM((tm, d), BF16), pltpu.VMEM((tm, d), F32)],
        compiler_params=_params("parallel", "parallel"),
        name="post",
    )(hf, hb, mid, mid, mid, ya, x, mod3, gn_w, n1_post, wm, wa, wo, n2_pre, n2_post, w1, w2)


def _layer(x, mod, norm1_pre, norm1_post, w_in, b_gates, conv_w, conv_b, mlstm_gn, attn_qnorm,
           attn_knorm, w_branch_m, w_branch_a, w_out, norm2_pre, norm2_post, w_mlp_in, w_mlp_out):
    bsz, s, d = x.shape
    inner = d
    n_gate = 4 * MLSTM_HEADS
    q_inner = d
    kv_inner = ATTN_KV_HEADS * ATTN_HEAD_DIM

    o_g = 4 * inner
    o_qa = o_g + n_gate
    o_ka = o_qa + q_inner
    o_va = o_ka + kv_inner
    o_br = o_va + kv_inner
    perm = _rope_head_perm()
    head_perm = lambda w: w.reshape(d, -1, ATTN_HEAD_DIM)[:, :, perm].reshape(d, -1)
    w_parts = [w_in[:, 0:o_g], w_in[:, o_br:o_br + 2 * d], head_perm(w_in[:, o_qa:o_va]), w_in[:, o_va:o_br]]
    w_main = jnp.concatenate([part.astype(BF16) for part in w_parts], axis=1)
    conv_cols, mid_cols = 2 * inner, inner + 2 * d
    hm = MLSTM_HEADS
    wg = w_in[:, o_g:o_qa]
    gate_cols = lambda a: (jnp.concatenate([a[..., 0:hm], a[..., 2 * hm:3 * hm]], -1),
                           jnp.concatenate([a[..., hm:2 * hm], a[..., 3 * hm:4 * hm]], -1))
    wi, wf = gate_cols(wg)
    bi, bf = gate_cols(b_gates)
    lane_pad = lambda a: jnp.pad(a, [(0, 0)] * (a.ndim - 1) + [(0, LANES - a.shape[-1])])
    w_gates = jnp.concatenate([lane_pad(wi), lane_pad(wf)], -1).astype(BF16)
    bias_gates = jnp.concatenate([lane_pad(bi), lane_pad(bf)], -1).reshape(1, 2 * LANES)

    mod3 = mod.reshape(bsz, 1, mod.shape[-1])
    cos, sin = _rope_tables(s)
    dh_m = inner // MLSTM_HEADS
    qk, vt_m, mid, q_rot, k_rot, v_t, gates = _in_proj(
        x, mod3, norm1_pre.reshape(1, d), w_main, w_gates, bias_gates, conv_w, conv_b, cos, sin,
        attn_qnorm[perm], attn_knorm[perm], conv_cols=conv_cols, mid_cols=mid_cols, qa_cols=q_inner,
        ka_cols=kv_inner, tm=min(s, 512), q_scale_m=dh_m ** -0.5,
        q_scale_a=ATTN_HEAD_DIM ** -0.5 * math.log2(math.e))

    h_f, h_b = _mlstm(qk, vt_m, gates, ln=min(s, 256))
    y_a = _attention(q_rot, k_rot, v_t, tq=min(s, 1024))

    return _post(h_f, h_b, mid, y_a, x, mod3, mlstm_gn.reshape(1, d), norm1_post.reshape(1, d),
                 w_branch_m.astype(BF16), w_branch_a.astype(BF16), w_out.astype(BF16),
                 norm2_pre.reshape(1, d), norm2_post.reshape(1, d), w_mlp_in.astype(BF16),
                 w_mlp_out.astype(BF16), o_block=0, bm_block=1, ba_block=2, tm=min(s, 512), ff_chunk=1024)


def kernel(x, c, w_ada, b_ada, norm1_pre, norm1_post, w_in, b_gates, conv_w, conv_b, mlstm_gn, attn_qnorm, attn_knorm, w_branch_m, w_branch_a, w_out, norm2_pre, norm2_post, w_mlp_in, w_mlp_out):
    for l in range(w_ada.shape[0]):
        mod = _ada(c, w_ada[l], b_ada[l])
        x = _layer(x, mod, norm1_pre[l], norm1_post[l], w_in[l], b_gates[l], conv_w[l], conv_b[l],
                   mlstm_gn[l], attn_qnorm[l], attn_knorm[l], w_branch_m[l], w_branch_a[l], w_out[l],
                   norm2_pre[l], norm2_post[l], w_mlp_in[l], w_mlp_out[l])
    return x
```

```python
import functools
import math

import jax
import jax.numpy as jnp
from jax import lax
from jax.experimental import pallas as pl
from jax.experimental.pallas import tpu as pltpu

EPS = 1e-6
GRID_W = 64
ROPE_THETA = 10000.0
MLSTM_HEADS = 4
MLSTM_CONV_W = 5
ATTN_HEAD_DIM = 128
ATTN_KV_HEADS = 2
LANES = 128
HALO = 16
ONES_ROWS = 16

F32 = jnp.float32
BF16 = jnp.bfloat16
VMEM_LIMIT = 56 * 1024 * 1024


def _params(*sem):
    return pltpu.CompilerParams(dimension_semantics=sem, vmem_limit_bytes=VMEM_LIMIT)


def _sigmoid(x):
    return 0.5 * jnp.tanh(0.5 * x) + 0.5


def _silu(x):
    half = 0.5 * x
    return half * jnp.tanh(half) + half


def _dot(a, b):
    return jnp.dot(a, b, preferred_element_type=F32)


def _dot_nt(a, b):
    return lax.dot_general(a, b, (((1,), (1,)), ((), ())), preferred_element_type=F32)


def _ada_kernel(c_ref, w_ref, b_ref, o_ref):
    c = c_ref[...]
    sc = _silu(c)
    o_ref[...] = jnp.dot(sc, w_ref[...], preferred_element_type=F32,
                         precision=lax.Precision.HIGHEST) + b_ref[...]


def _ada(c, w, b):
    bsz, d = c.shape
    n = w.shape[1]
    tn = 1024
    return pl.pallas_call(
        _ada_kernel,
        grid=(n // tn,),
        in_specs=[pl.BlockSpec((bsz, d), lambda j: (0, 0)),
                  pl.BlockSpec((d, tn), lambda j: (0, j)),
                  pl.BlockSpec((1, tn), lambda j: (0, j))],
        out_specs=pl.BlockSpec((bsz, tn), lambda j: (0, j)),
        out_shape=jax.ShapeDtypeStruct((bsz, n), F32),
        compiler_params=_params("parallel"),
        name="ada",
    )(c, w, b.reshape(1, n))


def _rope_head_perm():
    quarter = ATTN_HEAD_DIM // 4
    blocks = [0, 2, 1, 3]
    return jnp.concatenate([jnp.arange(b * quarter, (b + 1) * quarter) for b in blocks])


def _norm_rope(x, w, cos, sin):
    ms = jnp.mean(x * x, axis=-1, keepdims=True)
    y = x * lax.rsqrt(ms + EPS) * w
    return y * cos + pltpu.roll(y, LANES // 2, 1) * sin


def _in_proj_kernel(xp_ref, x_ref, xn_ref, mod_ref, nw_ref, w_ref, wg_ref, bg_ref, cw_ref, cb_ref,
                    cos_ref, sin_ref, qw_ref, kw_ref,
                    qk_ref, vm_ref, mid_ref, qo_ref, ko_ref, vt_ref, g_ref, h_scr, e_scr,
                    *, chunk, q_scale_m, q_scale_a):
    t = pl.program_id(1)
    nt = pl.num_programs(1)
    tm, d = x_ref.shape
    conv_cols = qk_ref.shape[1]
    dh_m = conv_cols // (2 * MLSTM_HEADS)
    mid_cols = mid_ref.shape[1]
    qa_cols = qo_ref.shape[1]
    ka_cols = ko_ref.shape[1]
    gain = nw_ref[...] * (1.0 + mod_ref[:, d:2 * d])
    shift = mod_ref[:, 0:d]

    def normed(xv):
        ms = jnp.mean(xv * xv, axis=-1, keepdims=True)
        return (xv * lax.rsqrt(ms + EPS) * gain + shift).astype(BF16)

    h_scr[0:HALO, :] = normed(xp_ref[...])
    h_scr[HALO:HALO + tm, :] = normed(x_ref[...])
    h_scr[HALO + tm:, :] = normed(xn_ref[...])
    centre = slice(HALO, HALO + tm)
    g_ref[...] = _dot(h_scr[centre, :], wg_ref[...]) + bg_ref[...]

    half = HALO // 2
    pad = MLSTM_CONV_W // 2
    has_prev = (t > 0).astype(F32)
    has_next = (t < nt - 1).astype(F32)

    def conv_chunk(c0):
        cs = slice(c0, c0 + chunk)
        e = _dot(h_scr[...], w_ref[:, cs])
        e_scr[0:half, :] = e[half:HALO] * has_prev
        e_scr[half:half + tm, :] = e[HALO:HALO + tm]
        e_scr[half + tm:, :] = e[HALO + tm:HALO + tm + half] * has_next
        acc = None
        for j in range(MLSTM_CONV_W):
            term = e_scr[half - pad + j:half - pad + j + tm, :] * cw_ref[j:j + 1, cs]
            acc = term if acc is None else acc + term
        y = acc + cb_ref[:, cs]
        y = _silu(y)
        if c0 < conv_cols // 2:
            y = y * q_scale_m
        qk_ref[:, cs] = y.astype(qk_ref.dtype)

    vm_base = conv_cols
    vm_cols = (vm_ref.shape[0] // (dh_m + ONES_ROWS)) * dh_m

    def vm_chunk(c0):
        a = _dot(h_scr[centre, :], w_ref[:, vm_base + c0:vm_base + c0 + chunk]).astype(vm_ref.dtype)
        for hh in range(chunk // dh_m):
            r0 = (c0 // dh_m + hh) * (dh_m + ONES_ROWS)
            for c in range(0, dh_m, LANES):
                vm_ref[r0 + c:r0 + c + LANES, :] = a[:, hh * dh_m + c:hh * dh_m + c + LANES].T
            vm_ref[r0 + dh_m:r0 + dh_m + ONES_ROWS, :] = jnp.ones((ONES_ROWS, tm), vm_ref.dtype)

    mid_base = vm_base + vm_cols

    def mid_chunk(c0):
        mid_ref[:, c0:c0 + chunk] = _dot(h_scr[centre, :],
                                         w_ref[:, mid_base + c0:mid_base + c0 + chunk]).astype(mid_ref.dtype)

    qa_base = mid_base + mid_cols
    cos = cos_ref[...]
    sin = sin_ref[...]

    def q_chunk(c0):
        a = _dot(h_scr[centre, :], w_ref[:, qa_base + c0:qa_base + c0 + chunk])
        for hh in range(chunk // LANES):
            y = _norm_rope(a[:, hh * LANES:(hh + 1) * LANES], qw_ref[...], cos, sin)
            qo_ref[:, c0 + hh * LANES:c0 + (hh + 1) * LANES] = (y * q_scale_a).astype(qo_ref.dtype)

    kv_base = qa_base + qa_cols

    def kv_chunk():
        a = _dot(h_scr[centre, :], w_ref[:, kv_base:kv_base + 2 * ka_cols])
        for g in range(ka_cols // LANES):
            y = _norm_rope(a[:, g * LANES:(g + 1) * LANES], kw_ref[...], cos, sin)
            ko_ref[:, g * LANES:(g + 1) * LANES] = y.astype(ko_ref.dtype)
            r0 = g * (LANES + ONES_ROWS)
            v = a[:, ka_cols + g * LANES:ka_cols + (g + 1) * LANES].astype(vt_ref.dtype)
            vt_ref[r0:r0 + LANES, :] = v.T
            vt_ref[r0 + LANES:r0 + LANES + ONES_ROWS, :] = jnp.ones((ONES_ROWS, tm), vt_ref.dtype)

    convs = [functools.partial(conv_chunk, c0) for c0 in range(0, conv_cols, chunk)]
    rots = [functools.partial(q_chunk, c0) for c0 in range(0, qa_cols, chunk)] + [kv_chunk]
    light = ([functools.partial(vm_chunk, c0) for c0 in range(0, vm_cols, chunk)]
             + [functools.partial(mid_chunk, c0) for c0 in range(0, mid_cols, chunk)])
    n = len(convs)
    for i in range(n):
        for task in convs[i:i + 1] + light[2 * i:2 * i + 1] + rots[i:i + 1] + light[2 * i + 1:2 * i + 2]:
            task()
    for task in rots[n:] + light[2 * n:]:
        task()


def _in_proj(x, mod3, norm_w, w_main, w_gates, b_gates, conv_w, conv_b, cos, sin, qn_w, kn_w,
             conv_cols, mid_cols, qa_cols, ka_cols, tm, q_scale_m, q_scale_a):
    bsz, s, d = x.shape
    n = w_main.shape[1]
    vm_cols = conv_cols // 2
    assert n == conv_cols + vm_cols + mid_cols + qa_cols + 2 * ka_cols
    ng = w_gates.shape[1]
    vm_rows = MLSTM_HEADS * (vm_cols // MLSTM_HEADS + ONES_ROWS)
    vt_rows = (ka_cols // LANES) * (LANES + ONES_ROWS)
    nh = s // HALO
    chunk = 512
    kern = functools.partial(_in_proj_kernel, chunk=chunk, q_scale_m=q_scale_m, q_scale_a=q_scale_a)
    tok = lambda b, t: (b, t, 0)
    const = lambda b, t: (0, 0)
    bf = lambda cols: jax.ShapeDtypeStruct((bsz, s, cols), BF16)
    return pl.pallas_call(
        kern,
        grid=(bsz, s // tm),
        in_specs=[pl.BlockSpec((None, HALO, d), lambda b, t: (b, jnp.maximum(t * (tm // HALO) - 1, 0), 0)),
                  pl.BlockSpec((None, tm, d), tok),
                  pl.BlockSpec((None, HALO, d),
                               lambda b, t: (b, jnp.minimum((t + 1) * (tm // HALO), nh - 1), 0)),
                  pl.BlockSpec((None, 1, mod3.shape[-1]), lambda b, t: (b, 0, 0)),
                  pl.BlockSpec((1, d), const),
                  pl.BlockSpec((d, n), const, pipeline_mode=pl.Buffered(1)),
                  pl.BlockSpec((d, ng), const),
                  pl.BlockSpec((1, ng), const),
                  pl.BlockSpec((MLSTM_CONV_W, conv_cols), const),
                  pl.BlockSpec((1, conv_cols), const),
                  pl.BlockSpec((tm, LANES), lambda b, t: (t, 0)),
                  pl.BlockSpec((tm, LANES), lambda b, t: (t, 0)),
                  pl.BlockSpec((1, LANES), const),
                  pl.BlockSpec((1, LANES), const)],
        out_specs=[pl.BlockSpec((None, tm, conv_cols), tok),
                   pl.BlockSpec((None, None, vm_rows, tm), lambda b, t: (b, t, 0, 0)),
                   pl.BlockSpec((None, tm, mid_cols), tok),
                   pl.BlockSpec((None, tm, qa_cols), tok),
                   pl.BlockSpec((None, tm, ka_cols), tok),
                   pl.BlockSpec((None, None, vt_rows, tm), lambda b, t: (b, t, 0, 0)),
                   pl.BlockSpec((None, tm, ng), tok)],
        out_shape=[bf(conv_cols),
                   jax.ShapeDtypeStruct((bsz, s // tm, vm_rows, tm), BF16),
                   bf(mid_cols), bf(qa_cols), bf(ka_cols),
                   jax.ShapeDtypeStruct((bsz, s // tm, vt_rows, tm), BF16),
                   jax.ShapeDtypeStruct((bsz, s, ng), F32)],
        scratch_shapes=[pltpu.VMEM((tm + 2 * HALO, d), BF16),
                        pltpu.VMEM((tm + HALO, chunk), F32)],
        compiler_params=_params("parallel", "parallel"),
        name="in_proj",
    )(x, x, x, mod3, norm_w, w_main, w_gates, b_gates, conv_w, conv_b.reshape(1, conv_cols), cos, sin,
      qn_w.reshape(1, LANES), kn_w.reshape(1, LANES))


def _mlstm_kernel(qkf_ref, qkb_ref, vtf_ref, vtb_ref, gf_ref, gb_ref, hf_ref, hb_ref, st_scr, m_scr):
    nb, ln = qkf_ref.shape[0], qkf_ref.shape[1]
    inner = qkf_ref.shape[2] // 2
    dh = inner // MLSTM_HEADS
    vrows = dh + ONES_ROWS
    nchain = 2 * MLSTM_HEADS

    @pl.when(pl.program_id(1) == 0)
    def _():
        st_scr[...] = jnp.zeros_like(st_scr)
        m_scr[...] = jnp.full_like(m_scr, -jnp.inf)

    row = lax.broadcasted_iota(jnp.int32, (ln, ln), 0)
    col = lax.broadcasted_iota(jnp.int32, (ln, ln), 1)
    lower = col <= row
    upper = col >= row

    def gate_terms(g_ref, tri):
        gi = g_ref[:, 0:LANES]
        gf = g_ref[:, LANES:2 * LANES]
        a = jnp.minimum(gf, 0.0) - jnp.log(1.0 + jnp.exp(-jnp.abs(gf)))
        tri_b = jnp.where(tri, 1.0, 0.0).astype(BF16)
        a1 = a.astype(BF16)
        r1 = a - a1.astype(F32)
        a2 = r1.astype(BF16)
        a3 = (r1 - a2.astype(F32)).astype(BF16)
        b = _dot(tri_b, a1) + _dot(tri_b, a2) + _dot(tri_b, a3)
        return gi - b, b.T

    gate = [(gate_terms(gf_ref.at[bb], lower), gate_terms(gb_ref.at[bb], upper)) for bb in range(nb)]

    for bb, direction in [(bb, direction) for bb in range(nb) for direction in range(2)]:
        qk_ref, vt_ref, h_ref = ((qkf_ref, vtf_ref, hf_ref), (qkb_ref, vtb_ref, hb_ref))[direction]
        qk_ref, vt_ref, h_ref = qk_ref.at[bb], vt_ref.at[bb], h_ref.at[bb]
        umat, btmat = gate[bb][direction]
        mask = (upper, lower)[direction]
        for head in range(MLSTM_HEADS):
            lane_j = direction * MLSTM_HEADS + head
            j = bb * nchain + lane_j
            q = qk_ref[:, head * dh:(head + 1) * dh]
            k = qk_ref[:, inner + head * dh:inner + (head + 1) * dh]
            vt = vt_ref[head * vrows:(head + 1) * vrows, :]
            u_col = umat[:, lane_j:lane_j + 1]
            b_row = btmat[lane_j:lane_j + 1, :]
            m_prev = m_scr[j:j + 1, 0:1]
            state = st_scr[j]

            u_m = jnp.where(mask, jnp.broadcast_to(u_col, (ln, ln)), -jnp.inf)
            g_row = jnp.maximum(jnp.max(u_m, axis=0, keepdims=True), m_prev)
            inter_qc = _dot_nt(state.astype(BF16), q)

            g_end = jnp.max(g_row, axis=1, keepdims=True)
            b_tot = jnp.min(b_row, axis=1, keepdims=True)
            kw = k.astype(F32) * jnp.exp(jnp.broadcast_to(u_col, (ln, dh)) - g_end)
            st_scr[j] = jnp.exp(m_prev - g_end) * state + _dot(vt, kw.astype(BF16))
            m_scr[j:j + 1, :] = jnp.broadcast_to(b_tot + g_end, (1, m_scr.shape[1]))

            w_t = _dot_nt(k, q) * jnp.exp(u_m - g_row)
            inter = jnp.exp(m_prev - g_row)
            tot = _dot(vt, w_t.astype(BF16)) + inter * inter_qc
            floor = jnp.exp(-(g_row + b_row))
            h_t = tot[0:dh, :] / jnp.maximum(jnp.abs(tot[dh:dh + 1, :]), floor)
            h_ref[:, head * dh:(head + 1) * dh] = h_t.T


def _mlstm(qk, vt, gates, ln):
    bsz, s, two_inner = qk.shape
    inner = two_inner // 2
    nc = s // ln
    ng = gates.shape[-1]
    dh = inner // MLSTM_HEADS
    nchain = 2 * MLSTM_HEADS
    vt_rows, tv = vt.shape[2], vt.shape[3]
    per = tv // ln
    nb = 2 if bsz % 2 == 0 else 1
    fwd = lambda b, i: (b, i, 0)
    bwd = lambda b, i: (b, nc - 1 - i, 0)
    return pl.pallas_call(
        _mlstm_kernel,
        grid=(bsz // nb, nc),
        in_specs=[pl.BlockSpec((nb, ln, two_inner), fwd),
                  pl.BlockSpec((nb, ln, two_inner), bwd),
                  pl.BlockSpec((nb, None, vt_rows, ln), lambda b, i: (b, i // per, 0, i % per)),
                  pl.BlockSpec((nb, None, vt_rows, ln),
                               lambda b, i: (b, (nc - 1 - i) // per, 0, (nc - 1 - i) % per)),
                  pl.BlockSpec((nb, ln, ng), fwd),
                  pl.BlockSpec((nb, ln, ng), bwd)],
        out_specs=[pl.BlockSpec((nb, ln, inner), fwd),
                   pl.BlockSpec((nb, ln, inner), bwd)],
        out_shape=[jax.ShapeDtypeStruct((bsz, s, inner), F32),
                   jax.ShapeDtypeStruct((bsz, s, inner), F32)],
        scratch_shapes=[pltpu.VMEM((nb * nchain, dh + ONES_ROWS, dh), F32),
                        pltpu.VMEM((nb * nchain, LANES), F32)],
        compiler_params=_params("parallel", "arbitrary"),
        name="mlstm",
    )(qk, qk, vt, vt, gates, gates)


def _rope_tab_kernel(freq_ref, cos_ref, sin_ref):
    n = cos_ref.shape[0]
    idx = lax.broadcasted_iota(jnp.int32, (n, LANES), 0).astype(F32)
    lane = lax.broadcasted_iota(jnp.int32, (n, LANES), 1)
    ang = idx * freq_ref[...]
    sign = jnp.where(lane < LANES // 2, -1.0, 1.0)
    cos_ref[...] = jnp.cos(ang)
    sin_ref[...] = jnp.sin(ang) * sign


def _rope_tables(s):
    n_freq = ATTN_HEAD_DIM // 4
    freqs = ROPE_THETA ** (-jnp.arange(n_freq, dtype=F32) / n_freq)
    freq_lanes = jnp.tile(freqs, 4).reshape(1, ATTN_HEAD_DIM)
    rows = s // GRID_W
    n = max(rows, GRID_W)
    assert n % 8 == 0
    cos_i, sin_i = pl.pallas_call(
        _rope_tab_kernel,
        grid=(1,),
        in_specs=[pl.BlockSpec((1, LANES), lambda t: (0, 0))],
        out_specs=[pl.BlockSpec((n, LANES), lambda t: (0, 0)),
                   pl.BlockSpec((n, LANES), lambda t: (0, 0))],
        out_shape=[jax.ShapeDtypeStruct((n, LANES), F32),
                   jax.ShapeDtypeStruct((n, LANES), F32)],
        compiler_params=_params("arbitrary"),
        name="rope_tab",
    )(freq_lanes)
    is_col = (jnp.arange(LANES) & (LANES // 4)) != 0

    def expand(tab):
        by_row = jnp.repeat(tab[:rows], GRID_W, axis=0)
        by_col = jnp.tile(tab[:GRID_W], (rows, 1))
        return jnp.where(is_col[None, :], by_col, by_row)

    return expand(cos_i), expand(sin_i)


def _attn_kernel(q_ref, k_ref, vt_ref, o_ref, qs_scr, m_scr, alpha_scr, acc_scr, st_scr, p_scr):
    tq = q_ref.shape[0]
    group = q_ref.shape[1] // LANES
    nk, tk = k_ref.shape[0], k_ref.shape[1]
    dh = LANES
    strip = 128

    for h in range(group):
        qs_scr[:, h * tq:(h + 1) * tq] = q_ref[:, h * LANES:(h + 1) * LANES].T
    m_scr[...] = jnp.full_like(m_scr, -jnp.inf)
    acc_scr[...] = jnp.zeros_like(acc_scr)

    def scores(j, slot):
        st_scr[slot] = _dot(k_ref[j], qs_scr[...]).astype(BF16)

    def softmax(slot):
        for h in range(group):
            cols = slice(h * tq, (h + 1) * tq)
            m_old = m_scr[:, cols]
            cmax = jnp.max(st_scr[slot, :, cols], axis=0, keepdims=True).astype(F32)
            m_new = jnp.maximum(m_old, cmax)
            m_b = m_new.astype(BF16)
            for r in range(0, tk, strip):
                p_scr[slot, r:r + strip, cols] = jnp.exp2(st_scr[slot, r:r + strip, cols] - m_b)
            alpha_scr[slot, :, cols] = jnp.exp2(m_old - m_new)
            m_scr[:, cols] = m_new

    def pv(j, slot):
        acc_scr[...] = alpha_scr[slot] * acc_scr[...] + _dot(vt_ref[j], p_scr[slot])

    assert nk % 2 == 0
    scores(0, 0)
    scores(1, 1)
    softmax(0)

    def body(i, carry):
        j = 2 * i + 1
        scores(j + 1, 0)
        softmax(1)
        pv(j - 1, 0)
        scores(j + 2, 1)
        softmax(0)
        pv(j, 1)
        return carry

    lax.fori_loop(0, (nk - 2) // 2, body, 0)
    softmax(1)
    pv(nk - 2, 0)
    pv(nk - 1, 1)
    for h in range(group):
        cols = slice(h * tq, (h + 1) * tq)
        out_t = acc_scr[0:dh, cols] / acc_scr[dh:dh + 1, cols]
        o_ref[:, h * LANES:(h + 1) * LANES] = out_t.T.astype(o_ref.dtype)


def _attention(q, k, vt, tq):
    bsz, s, q_inner = q.shape
    dh = ATTN_HEAD_DIM
    nk, tk = vt.shape[1], vt.shape[3]
    kvh = k.shape[-1] // dh
    gw = q_inner // kvh
    group = gw // dh
    vrows = dh + ONES_ROWS
    k4 = k.reshape(bsz, nk, tk, kvh * dh)
    return pl.pallas_call(
        _attn_kernel,
        grid=(bsz, kvh, s // tq),
        in_specs=[pl.BlockSpec((None, tq, gw), lambda b, g, i: (b, i, g)),
                  pl.BlockSpec((None, nk, tk, dh), lambda b, g, i: (b, 0, 0, g)),
                  pl.BlockSpec((None, nk, vrows, tk), lambda b, g, i: (b, 0, g, 0))],
        out_specs=pl.BlockSpec((None, tq, gw), lambda b, g, i: (b, i, g)),
        out_shape=jax.ShapeDtypeStruct((bsz, s, q_inner), BF16),
        scratch_shapes=[pltpu.VMEM((dh, group * tq), BF16),
                        pltpu.VMEM((1, group * tq), F32),
                        pltpu.VMEM((2, 1, group * tq), F32),
                        pltpu.VMEM((vrows, group * tq), F32),
                        pltpu.VMEM((2, tk, group * tq), BF16),
                        pltpu.VMEM((2, tk, group * tq), BF16)],
        compiler_params=_params("parallel", "parallel", "parallel"),
        name="attn",
    )(q, k4, vt)


def _post_kernel(hf_ref, hb_ref, o_ref, bm_ref, ba_ref, ya_ref, x_ref, mod_ref, gn_ref, n1_ref, wm_ref,
                 wa_ref, wo_ref, n2a_ref, n2b_ref, w1_ref, w2_ref, out_ref, ym_scr, x1_scr, *, ff_chunk):
    d = x_ref.shape[-1]
    dh = d // MLSTM_HEADS
    pa = _dot(ya_ref[...], wa_ref[...])
    pm = None
    for head in range(MLSTM_HEADS):
        sl = slice(head * dh, (head + 1) * dh)
        h = hf_ref[:, sl] + hb_ref[:, sl]
        mu = jnp.mean(h, axis=-1, keepdims=True)
        hc = h - mu
        var = jnp.mean(hc * hc, axis=-1, keepdims=True)
        hn = hc * lax.rsqrt(var + EPS) * gn_ref[:, sl]
        ym_scr[:, sl] = (hn * _sigmoid(o_ref[:, sl].astype(F32))).astype(BF16)
        part = _dot(ym_scr[:, sl], wm_ref[sl, :])
        pm = part if pm is None else pm + part
    y = _sigmoid(bm_ref[...].astype(F32)) * pm + _sigmoid(ba_ref[...].astype(F32)) * pa
    y2 = _dot(y.astype(BF16), wo_ref[...])
    ms = jnp.mean(y2 * y2, axis=-1, keepdims=True)
    x1_scr[...] = x_ref[...] + mod_ref[:, 2 * d:3 * d] * (y2 * lax.rsqrt(ms + EPS) * n1_ref[...])

    x1 = x1_scr[...]
    ms1 = jnp.mean(x1 * x1, axis=-1, keepdims=True)
    shift = mod_ref[:, 3 * d:4 * d]
    scale = mod_ref[:, 4 * d:5 * d]
    gate = mod_ref[:, 5 * d:6 * d]
    h2 = (x1 * lax.rsqrt(ms1 + EPS) * n2a_ref[...] * (1.0 + scale) + shift).astype(BF16)
    acc = None
    for c in range(w1_ref.shape[1] // ff_chunk):
        sl = slice(c * ff_chunk, (c + 1) * ff_chunk)
        u = jnp.maximum(_dot(h2, w1_ref[:, sl]), 0.0)
        part = _dot((u * u).astype(BF16), w2_ref[sl, :])
        acc = part if acc is None else acc + part
    ms2 = jnp.mean(acc * acc, axis=-1, keepdims=True)
    out_ref[...] = x1_scr[...] + gate * (acc * lax.rsqrt(ms2 + EPS) * n2b_ref[...])


def _post(hf, hb, mid, ya, x, mod3, gn_w, n1_post, wm, wa, wo, n2_pre, n2_post, w1, w2,
          o_block, bm_block, ba_block, tm, ff_chunk):
    bsz, s, d = x.shape
    ff = w1.shape[1]
    tok = lambda b, t: (b, t, 0)
    const = lambda b, t: (0, 0)
    resident = lambda shape: pl.BlockSpec(shape, const, pipeline_mode=pl.Buffered(1))
    kern = functools.partial(_post_kernel, ff_chunk=ff_chunk)
    return pl.pallas_call(
        kern,
        grid=(bsz, s // tm),
        in_specs=[pl.BlockSpec((None, tm, d), tok),
                  pl.BlockSpec((None, tm, d), tok),
                  pl.BlockSpec((None, tm, d), lambda b, t: (b, t, o_block)),
                  pl.BlockSpec((None, tm, d), lambda b, t: (b, t, bm_block)),
                  pl.BlockSpec((None, tm, d), lambda b, t: (b, t, ba_block)),
                  pl.BlockSpec((None, tm, d), tok),
                  pl.BlockSpec((None, tm, d), tok),
                  pl.BlockSpec((None, 1, mod3.shape[-1]), lambda b, t: (b, 0, 0)),
                  pl.BlockSpec((1, d), const),
                  pl.BlockSpec((1, d), const),
                  resident((d, d)), resident((d, d)), resident((d, d)),
                  pl.BlockSpec((1, d), const),
                  pl.BlockSpec((1, d), const),
                  resident((d, ff)), resident((ff, d))],
        out_specs=pl.BlockSpec((None, tm, d), tok),
        out_shape=jax.ShapeDtypeStruct((bsz, s, d), F32),
        scratch_shapes=[pltpu.VMEM((tm, d), BF16), pltpu.VMEM((tm, d), F32)],
        compiler_params=_params("parallel", "parallel"),
        name="post",
    )(hf, hb, mid, mid, mid, ya, x, mod3, gn_w, n1_post, wm, wa, wo, n2_pre, n2_post, w1, w2)


def _layer(x, mod, norm1_pre, norm1_post, w_in, b_gates, conv_w, conv_b, mlstm_gn, attn_qnorm,
           attn_knorm, w_branch_m, w_branch_a, w_out, norm2_pre, norm2_post, w_mlp_in, w_mlp_out):
    bsz, s, d = x.shape
    inner = d
    n_gate = 4 * MLSTM_HEADS
    q_inner = d
    kv_inner = ATTN_KV_HEADS * ATTN_HEAD_DIM

    o_g = 4 * inner
    o_qa = o_g + n_gate
    o_ka = o_qa + q_inner
    o_va = o_ka + kv_inner
    o_br = o_va + kv_inner
    perm = _rope_head_perm()
    head_perm = lambda w: w.reshape(d, -1, ATTN_HEAD_DIM)[:, :, perm].reshape(d, -1)
    w_parts = [w_in[:, 0:o_g], w_in[:, o_br:o_br + 2 * d], head_perm(w_in[:, o_qa:o_va]), w_in[:, o_va:o_br]]
    w_main = jnp.concatenate([part.astype(BF16) for part in w_parts], axis=1)
    conv_cols, mid_cols = 2 * inner, inner + 2 * d
    hm = MLSTM_HEADS
    wg = w_in[:, o_g:o_qa]
    gate_cols = lambda a: (jnp.concatenate([a[..., 0:hm], a[..., 2 * hm:3 * hm]], -1),
                           jnp.concatenate([a[..., hm:2 * hm], a[..., 3 * hm:4 * hm]], -1))
    wi, wf = gate_cols(wg)
    bi, bf = gate_cols(b_gates)
    lane_pad = lambda a: jnp.pad(a, [(0, 0)] * (a.ndim - 1) + [(0, LANES - a.shape[-1])])
    w_gates = jnp.concatenate([lane_pad(wi), lane_pad(wf)], -1).astype(BF16)
    bias_gates = jnp.concatenate([lane_pad(bi), lane_pad(bf)], -1).reshape(1, 2 * LANES)

    mod3 = mod.reshape(bsz, 1, mod.shape[-1])
    cos, sin = _rope_tables(s)
    dh_m = inner // MLSTM_HEADS
    qk, vt_m, mid, q_rot, k_rot, v_t, gates = _in_proj(
        x, mod3, norm1_pre.reshape(1, d), w_main, w_gates, bias_gates, conv_w, conv_b, cos, sin,
        attn_qnorm[perm], attn_knorm[perm], conv_cols=conv_cols, mid_cols=mid_cols, qa_cols=q_inner,
        ka_cols=kv_inner, tm=min(s, 512), q_scale_m=dh_m ** -0.5,
        q_scale_a=ATTN_HEAD_DIM ** -0.5 * math.log2(math.e))

    h_f, h_b = _mlstm(qk, vt_m, gates, ln=min(s, 256))
    y_a = _attention(q_rot, k_rot, v_t, tq=min(s, 1024))

    return _post(h_f, h_b, mid, y_a, x, mod3, mlstm_gn.reshape(1, d), norm1_post.reshape(1, d),
                 w_branch_m.astype(BF16), w_branch_a.astype(BF16), w_out.astype(BF16),
                 norm2_pre.reshape(1, d), norm2_post.reshape(1, d), w_mlp_in.astype(BF16),
                 w_mlp_out.astype(BF16), o_block=0, bm_block=1, ba_block=2, tm=min(s, 512), ff_chunk=1024)


def kernel(x, c, w_ada, b_ada, norm1_pre, norm1_post, w_in, b_gates, conv_w, conv_b, mlstm_gn, attn_qnorm, attn_knorm, w_branch_m, w_branch_a, w_out, norm2_pre, norm2_post, w_mlp_in, w_mlp_out):
    for l in range(w_ada.shape[0]):
        mod = _ada(c, w_ada[l], b_ada[l])
        x = _layer(x, mod, norm1_pre[l], norm1_post[l], w_in[l], b_gates[l], conv_w[l], conv_b[l],
                   mlstm_gn[l], attn_qnorm[l], attn_knorm[l], w_branch_m[l], w_branch_a[l], w_out[l],
                   norm2_pre[l], norm2_post[l], w_mlp_in[l], w_mlp_out[l])
    return x
```

```python
import functools
import math

import jax
import jax.numpy as jnp
from jax import lax
from jax.experimental import pallas as pl
from jax.experimental.pallas import tpu as pltpu

EPS = 1e-6
GRID_W = 64
ROPE_THETA = 10000.0
MLSTM_HEADS = 4
MLSTM_CONV_W = 5
ATTN_HEAD_DIM = 128
ATTN_KV_HEADS = 2
LANES = 128
HALO = 16
ONES_ROWS = 16

F32 = jnp.float32
BF16 = jnp.bfloat16
VMEM_LIMIT = 56 * 1024 * 1024


def _params(*sem):
    return pltpu.CompilerParams(dimension_semantics=sem, vmem_limit_bytes=VMEM_LIMIT)


def _sigmoid(x):
    return 0.5 * jnp.tanh(0.5 * x) + 0.5


def _silu(x):
    half = 0.5 * x
    return half * jnp.tanh(half) + half


def _dot(a, b):
    return jnp.dot(a, b, preferred_element_type=F32)


def _dot_nt(a, b):
    return lax.dot_general(a, b, (((1,), (1,)), ((), ())), preferred_element_type=F32)


def _ada_kernel(c_ref, w_ref, b_ref, o_ref):
    sc = _silu(c_ref[...])
    w = w_ref[...]
    sc_hi = sc.astype(BF16)
    sc_lo = (sc - sc_hi.astype(F32)).astype(BF16)
    w_hi = w.astype(BF16)
    w_lo = (w - w_hi.astype(F32)).astype(BF16)
    o_ref[...] = _dot(sc_hi, w_hi) + _dot(sc_hi, w_lo) + _dot(sc_lo, w_hi) + b_ref[...]


def _ada(c, w, b):
    bsz, d = c.shape
    n = w.shape[1]
    tn = 1024
    return pl.pallas_call(
        _ada_kernel,
        grid=(n // tn,),
        in_specs=[pl.BlockSpec((bsz, d), lambda j: (0, 0)),
                  pl.BlockSpec((d, tn), lambda j: (0, j)),
                  pl.BlockSpec((1, tn), lambda j: (0, j))],
        out_specs=pl.BlockSpec((bsz, tn), lambda j: (0, j)),
        out_shape=jax.ShapeDtypeStruct((bsz, n), F32),
        compiler_params=_params("parallel"),
        name="ada",
    )(c, w, b.reshape(1, n))


def _rope_head_perm():
    quarter = ATTN_HEAD_DIM // 4
    blocks = [0, 2, 1, 3]
    return jnp.concatenate([jnp.arange(b * quarter, (b + 1) * quarter) for b in blocks])


def _norm_rope(x, w, cos, sin):
    ms = jnp.mean(x * x, axis=-1, keepdims=True)
    y = x * lax.rsqrt(ms + EPS) * w
    return y * cos + pltpu.roll(y, LANES // 2, 1) * sin


def _in_proj_kernel(xp_ref, x_ref, xn_ref, mod_ref, nw_ref, w_ref, wg_ref, bg_ref, cw_ref, cb_ref,
                    cos_ref, sin_ref, qw_ref, kw_ref,
                    qk_ref, vm_ref, mid_ref, qo_ref, ko_ref, vt_ref, g_ref, h_scr, e_scr,
                    *, chunk, q_scale_m, q_scale_a):
    t = pl.program_id(1)
    nt = pl.num_programs(1)
    tm, d = x_ref.shape
    conv_cols = qk_ref.shape[1]
    dh_m = conv_cols // (2 * MLSTM_HEADS)
    mid_cols = mid_ref.shape[1]
    qa_cols = qo_ref.shape[1]
    ka_cols = ko_ref.shape[1]
    gain = nw_ref[...] * (1.0 + mod_ref[:, d:2 * d])
    shift = mod_ref[:, 0:d]

    def normed(xv):
        ms = jnp.mean(xv * xv, axis=-1, keepdims=True)
        return (xv * lax.rsqrt(ms + EPS) * gain + shift).astype(BF16)

    h_scr[0:HALO, :] = normed(xp_ref[...])
    h_scr[HALO:HALO + tm, :] = normed(x_ref[...])
    h_scr[HALO + tm:, :] = normed(xn_ref[...])
    centre = slice(HALO, HALO + tm)
    g_ref[...] = _dot(h_scr[centre, :], wg_ref[...]) + bg_ref[...]

    half = HALO // 2
    pad = MLSTM_CONV_W // 2
    has_prev = (t > 0).astype(F32)
    has_next = (t < nt - 1).astype(F32)

    def conv_chunk(c0):
        cs = slice(c0, c0 + chunk)
        e = _dot(h_scr[...], w_ref[:, cs])
        e_scr[0:half, :] = e[half:HALO] * has_prev
        e_scr[half:half + tm, :] = e[HALO:HALO + tm]
        e_scr[half + tm:, :] = e[HALO + tm:HALO + tm + half] * has_next
        acc = None
        for j in range(MLSTM_CONV_W):
            term = e_scr[half - pad + j:half - pad + j + tm, :] * cw_ref[j:j + 1, cs]
            acc = term if acc is None else acc + term
        y = acc + cb_ref[:, cs]
        y = _silu(y)
        if c0 < conv_cols // 2:
            y = y * q_scale_m
        qk_ref[:, cs] = y.astype(qk_ref.dtype)

    vm_base = conv_cols
    vm_cols = (vm_ref.shape[0] // (dh_m + ONES_ROWS)) * dh_m

    def vm_chunk(c0):
        a = _dot(h_scr[centre, :], w_ref[:, vm_base + c0:vm_base + c0 + chunk]).astype(vm_ref.dtype)
        for hh in range(chunk // dh_m):
            r0 = (c0 // dh_m + hh) * (dh_m + ONES_ROWS)
            for c in range(0, dh_m, LANES):
                vm_ref[r0 + c:r0 + c + LANES, :] = a[:, hh * dh_m + c:hh * dh_m + c + LANES].T
            vm_ref[r0 + dh_m:r0 + dh_m + ONES_ROWS, :] = jnp.ones((ONES_ROWS, tm), vm_ref.dtype)

    mid_base = vm_base + vm_cols

    def mid_chunk(c0):
        mid_ref[:, c0:c0 + chunk] = _dot(h_scr[centre, :],
                                         w_ref[:, mid_base + c0:mid_base + c0 + chunk]).astype(mid_ref.dtype)

    qa_base = mid_base + mid_cols
    cos = cos_ref[...]
    sin = sin_ref[...]

    def q_chunk(c0):
        a = _dot(h_scr[centre, :], w_ref[:, qa_base + c0:qa_base + c0 + chunk])
        for hh in range(chunk // LANES):
            y = _norm_rope(a[:, hh * LANES:(hh + 1) * LANES], qw_ref[...], cos, sin)
            qo_ref[:, c0 + hh * LANES:c0 + (hh + 1) * LANES] = (y * q_scale_a).astype(qo_ref.dtype)

    kv_base = qa_base + qa_cols

    def kv_chunk():
        a = _dot(h_scr[centre, :], w_ref[:, kv_base:kv_base + 2 * ka_cols])
        for g in range(ka_cols // LANES):
            y = _norm_rope(a[:, g * LANES:(g + 1) * LANES], kw_ref[...], cos, sin)
            ko_ref[:, g * LANES:(g + 1) * LANES] = y.astype(ko_ref.dtype)
            r0 = g * (LANES + ONES_ROWS)
            v = a[:, ka_cols + g * LANES:ka_cols + (g + 1) * LANES].astype(vt_ref.dtype)
            vt_ref[r0:r0 + LANES, :] = v.T
            vt_ref[r0 + LANES:r0 + LANES + ONES_ROWS, :] = jnp.ones((ONES_ROWS, tm), vt_ref.dtype)

    convs = [functools.partial(conv_chunk, c0) for c0 in range(0, conv_cols, chunk)]
    rots = [functools.partial(q_chunk, c0) for c0 in range(0, qa_cols, chunk)] + [kv_chunk]
    light = ([functools.partial(vm_chunk, c0) for c0 in range(0, vm_cols, chunk)]
             + [functools.partial(mid_chunk, c0) for c0 in range(0, mid_cols, chunk)])
    n = len(convs)
    for i in range(n):
        for task in convs[i:i + 1] + light[2 * i:2 * i + 1] + rots[i:i + 1] + light[2 * i + 1:2 * i + 2]:
            task()
    for task in rots[n:] + light[2 * n:]:
        task()


def _in_proj(x, mod3, norm_w, w_main, w_gates, b_gates, conv_w, conv_b, cos, sin, qn_w, kn_w,
             conv_cols, mid_cols, qa_cols, ka_cols, tm, q_scale_m, q_scale_a):
    bsz, s, d = x.shape
    n = w_main.shape[1]
    vm_cols = conv_cols // 2
    assert n == conv_cols + vm_cols + mid_cols + qa_cols + 2 * ka_cols
    ng = w_gates.shape[1]
    vm_rows = MLSTM_HEADS * (vm_cols // MLSTM_HEADS + ONES_ROWS)
    vt_rows = (ka_cols // LANES) * (LANES + ONES_ROWS)
    nh = s // HALO
    chunk = 512
    kern = functools.partial(_in_proj_kernel, chunk=chunk, q_scale_m=q_scale_m, q_scale_a=q_scale_a)
    tok = lambda b, t: (b, t, 0)
    const = lambda b, t: (0, 0)
    bf = lambda cols: jax.ShapeDtypeStruct((bsz, s, cols), BF16)
    return pl.pallas_call(
        kern,
        grid=(bsz, s // tm),
        in_specs=[pl.BlockSpec((None, HALO, d), lambda b, t: (b, jnp.maximum(t * (tm // HALO) - 1, 0), 0)),
                  pl.BlockSpec((None, tm, d), tok),
                  pl.BlockSpec((None, HALO, d),
                               lambda b, t: (b, jnp.minimum((t + 1) * (tm // HALO), nh - 1), 0)),
                  pl.BlockSpec((None, 1, mod3.shape[-1]), lambda b, t: (b, 0, 0)),
                  pl.BlockSpec((1, d), const),
                  pl.BlockSpec((d, n), const, pipeline_mode=pl.Buffered(1)),
                  pl.BlockSpec((d, ng), const),
                  pl.BlockSpec((1, ng), const),
                  pl.BlockSpec((MLSTM_CONV_W, conv_cols), const),
                  pl.BlockSpec((1, conv_cols), const),
                  pl.BlockSpec((tm, LANES), lambda b, t: (t, 0)),
                  pl.BlockSpec((tm, LANES), lambda b, t: (t, 0)),
                  pl.BlockSpec((1, LANES), const),
                  pl.BlockSpec((1, LANES), const)],
        out_specs=[pl.BlockSpec((None, tm, conv_cols), tok),
                   pl.BlockSpec((None, None, vm_rows, tm), lambda b, t: (b, t, 0, 0)),
                   pl.BlockSpec((None, tm, mid_cols), tok),
                   pl.BlockSpec((None, tm, qa_cols), tok),
                   pl.BlockSpec((None, tm, ka_cols), tok),
                   pl.BlockSpec((None, None, vt_rows, tm), lambda b, t: (b, t, 0, 0)),
                   pl.BlockSpec((None, tm, ng), tok)],
        out_shape=[bf(conv_cols),
                   jax.ShapeDtypeStruct((bsz, s // tm, vm_rows, tm), BF16),
                   bf(mid_cols), bf(qa_cols), bf(ka_cols),
                   jax.ShapeDtypeStruct((bsz, s // tm, vt_rows, tm), BF16),
                   jax.ShapeDtypeStruct((bsz, s, ng), F32)],
        scratch_shapes=[pltpu.VMEM((tm + 2 * HALO, d), BF16),
                        pltpu.VMEM((tm + HALO, chunk), F32)],
        compiler_params=_params("parallel", "parallel"),
        name="in_proj",
    )(x, x, x, mod3, norm_w, w_main, w_gates, b_gates, conv_w, conv_b.reshape(1, conv_cols), cos, sin,
      qn_w.reshape(1, LANES), kn_w.reshape(1, LANES))


def _mlstm_kernel(qkf_ref, qkb_ref, vtf_ref, vtb_ref, gf_ref, gb_ref, hf_ref, hb_ref, st_scr, m_scr):
    nb, ln = qkf_ref.shape[0], qkf_ref.shape[1]
    inner = qkf_ref.shape[2] // 2
    dh = inner // MLSTM_HEADS
    vrows = dh + ONES_ROWS
    nchain = 2 * MLSTM_HEADS

    @pl.when(pl.program_id(1) == 0)
    def _():
        st_scr[...] = jnp.zeros_like(st_scr)
        m_scr[...] = jnp.full_like(m_scr, -jnp.inf)

    row = lax.broadcasted_iota(jnp.int32, (ln, ln), 0)
    col = lax.broadcasted_iota(jnp.int32, (ln, ln), 1)
    lower = col <= row
    upper = col >= row

    def gate_terms(g_ref, tri):
        gi = g_ref[:, 0:LANES]
        gf = g_ref[:, LANES:2 * LANES]
        a = jnp.minimum(gf, 0.0) - jnp.log(1.0 + jnp.exp(-jnp.abs(gf)))
        tri_b = jnp.where(tri, 1.0, 0.0).astype(BF16)
        a1 = a.astype(BF16)
        r1 = a - a1.astype(F32)
        a2 = r1.astype(BF16)
        a3 = (r1 - a2.astype(F32)).astype(BF16)
        b = _dot(tri_b, a1) + _dot(tri_b, a2) + _dot(tri_b, a3)
        return gi - b, b.T

    gate = [(gate_terms(gf_ref.at[bb], lower), gate_terms(gb_ref.at[bb], upper)) for bb in range(nb)]

    for bb, direction in [(bb, direction) for bb in range(nb) for direction in range(2)]:
        qk_ref, vt_ref, h_ref = ((qkf_ref, vtf_ref, hf_ref), (qkb_ref, vtb_ref, hb_ref))[direction]
        qk_ref, vt_ref, h_ref = qk_ref.at[bb], vt_ref.at[bb], h_ref.at[bb]
        umat, btmat = gate[bb][direction]
        mask = (upper, lower)[direction]
        for head in range(MLSTM_HEADS):
            lane_j = direction * MLSTM_HEADS + head
            j = bb * nchain + lane_j
            q = qk_ref[:, head * dh:(head + 1) * dh]
            k = qk_ref[:, inner + head * dh:inner + (head + 1) * dh]
            vt = vt_ref[head * vrows:(head + 1) * vrows, :]
            u_col = umat[:, lane_j:lane_j + 1]
            b_row = btmat[lane_j:lane_j + 1, :]
            m_prev = m_scr[j:j + 1, 0:1]
            state = st_scr[j]

            u_m = jnp.where(mask, jnp.broadcast_to(u_col, (ln, ln)), -jnp.inf)
            g_row = jnp.maximum(jnp.max(u_m, axis=0, keepdims=True), m_prev)
            inter_qc = _dot_nt(state.astype(BF16), q)

            g_end = jnp.max(g_row, axis=1, keepdims=True)
            b_tot = jnp.min(b_row, axis=1, keepdims=True)
            kw = k.astype(F32) * jnp.exp(jnp.broadcast_to(u_col, (ln, dh)) - g_end)
            st_scr[j] = jnp.exp(m_prev - g_end) * state + _dot(vt, kw.astype(BF16))
            m_scr[j:j + 1, :] = jnp.broadcast_to(b_tot + g_end, (1, m_scr.shape[1]))

            w_t = _dot_nt(k, q) * jnp.exp(u_m - g_row)
            inter = jnp.exp(m_prev - g_row)
            tot = _dot(vt, w_t.astype(BF16)) + inter * inter_qc
            floor = jnp.exp(-(g_row + b_row))
            h_t = tot[0:dh, :] / jnp.maximum(jnp.abs(tot[dh:dh + 1, :]), floor)
            h_ref[:, head * dh:(head + 1) * dh] = h_t.T


def _mlstm(qk, vt, gates, ln):
    bsz, s, two_inner = qk.shape
    inner = two_inner // 2
    nc = s // ln
    ng = gates.shape[-1]
    dh = inner // MLSTM_HEADS
    nchain = 2 * MLSTM_HEADS
    vt_rows, tv = vt.shape[2], vt.shape[3]
    per = tv // ln
    nb = 2 if bsz % 2 == 0 else 1
    fwd = lambda b, i: (b, i, 0)
    bwd = lambda b, i: (b, nc - 1 - i, 0)
    return pl.pallas_call(
        _mlstm_kernel,
        grid=(bsz // nb, nc),
        in_specs=[pl.BlockSpec((nb, ln, two_inner), fwd),
                  pl.BlockSpec((nb, ln, two_inner), bwd),
                  pl.BlockSpec((nb, None, vt_rows, ln), lambda b, i: (b, i // per, 0, i % per)),
                  pl.BlockSpec((nb, None, vt_rows, ln),
                               lambda b, i: (b, (nc - 1 - i) // per, 0, (nc - 1 - i) % per)),
                  pl.BlockSpec((nb, ln, ng), fwd),
                  pl.BlockSpec((nb, ln, ng), bwd)],
        out_specs=[pl.BlockSpec((nb, ln, inner), fwd),
                   pl.BlockSpec((nb, ln, inner), bwd)],
        out_shape=[jax.ShapeDtypeStruct((bsz, s, inner), F32),
                   jax.ShapeDtypeStruct((bsz, s, inner), F32)],
        scratch_shapes=[pltpu.VMEM((nb * nchain, dh + ONES_ROWS, dh), F32),
                        pltpu.VMEM((nb * nchain, LANES), F32)],
        compiler_params=_params("parallel", "arbitrary"),
        name="mlstm",
    )(qk, qk, vt, vt, gates, gates)


def _rope_tab_kernel(freq_ref, cos_ref, sin_ref):
    n = cos_ref.shape[0]
    idx = lax.broadcasted_iota(jnp.int32, (n, LANES), 0).astype(F32)
    lane = lax.broadcasted_iota(jnp.int32, (n, LANES), 1)
    ang = idx * freq_ref[...]
    sign = jnp.where(lane < LANES // 2, -1.0, 1.0)
    cos_ref[...] = jnp.cos(ang)
    sin_ref[...] = jnp.sin(ang) * sign


def _rope_tables(s):
    n_freq = ATTN_HEAD_DIM // 4
    freqs = ROPE_THETA ** (-jnp.arange(n_freq, dtype=F32) / n_freq)
    freq_lanes = jnp.tile(freqs, 4).reshape(1, ATTN_HEAD_DIM)
    rows = s // GRID_W
    n = max(rows, GRID_W)
    assert n % 8 == 0
    cos_i, sin_i = pl.pallas_call(
        _rope_tab_kernel,
        grid=(1,),
        in_specs=[pl.BlockSpec((1, LANES), lambda t: (0, 0))],
        out_specs=[pl.BlockSpec((n, LANES), lambda t: (0, 0)),
                   pl.BlockSpec((n, LANES), lambda t: (0, 0))],
        out_shape=[jax.ShapeDtypeStruct((n, LANES), F32),
                   jax.ShapeDtypeStruct((n, LANES), F32)],
        compiler_params=_params("arbitrary"),
        name="rope_tab",
    )(freq_lanes)
    is_col = (jnp.arange(LANES) & (LANES // 4)) != 0

    def expand(tab):
        by_row = jnp.repeat(tab[:rows], GRID_W, axis=0)
        by_col = jnp.tile(tab[:GRID_W], (rows, 1))
        return jnp.where(is_col[None, :], by_col, by_row)

    return expand(cos_i), expand(sin_i)


def _attn_kernel(q_ref, k_ref, vt_ref, o_ref, qs_scr, m_scr, alpha_scr, acc_scr, st_scr, p_scr):
    tq = q_ref.shape[0]
    group = q_ref.shape[1] // LANES
    nk, tk = k_ref.shape[0], k_ref.shape[1]
    dh = LANES
    strip = 128

    for h in range(group):
        qs_scr[:, h * tq:(h + 1) * tq] = q_ref[:, h * LANES:(h + 1) * LANES].T
    m_scr[...] = jnp.full_like(m_scr, -jnp.inf)
    acc_scr[...] = jnp.zeros_like(acc_scr)

    def scores(j, slot):
        st_scr[slot] = _dot(k_ref[j], qs_scr[...]).astype(BF16)

    def softmax(slot):
        for h in range(group):
            cols = slice(h * tq, (h + 1) * tq)
            m_old = m_scr[:, cols]
            cmax = jnp.max(st_scr[slot, :, cols], axis=0, keepdims=True).astype(F32)
            m_new = jnp.maximum(m_old, cmax)
            m_b = m_new.astype(BF16)
            for r in range(0, tk, strip):
                p_scr[slot, r:r + strip, cols] = jnp.exp2(st_scr[slot, r:r + strip, cols] - m_b)
            alpha_scr[slot, :, cols] = jnp.exp2(m_old - m_new)
            m_scr[:, cols] = m_new

    def pv(j, slot):
        acc_scr[...] = alpha_scr[slot] * acc_scr[...] + _dot(vt_ref[j], p_scr[slot])

    assert nk % 2 == 0
    scores(0, 0)
    scores(1, 1)
    softmax(0)

    def body(i, carry):
        j = 2 * i + 1
        scores(j + 1, 0)
        softmax(1)
        pv(j - 1, 0)
        scores(j + 2, 1)
        softmax(0)
        pv(j, 1)
        return carry

    lax.fori_loop(0, (nk - 2) // 2, body, 0)
    softmax(1)
    pv(nk - 2, 0)
    pv(nk - 1, 1)
    for h in range(group):
        cols = slice(h * tq, (h + 1) * tq)
        out_t = acc_scr[0:dh, cols] / acc_scr[dh:dh + 1, cols]
        o_ref[:, h * LANES:(h + 1) * LANES] = out_t.T.astype(o_ref.dtype)


def _attention(q, k, vt, tq):
    bsz, s, q_inner = q.shape
    dh = ATTN_HEAD_DIM
    nk, tk = vt.shape[1], vt.shape[3]
    kvh = k.shape[-1] // dh
    gw = q_inner // kvh
    group = gw // dh
    vrows = dh + ONES_ROWS
    k4 = k.reshape(bsz, nk, tk, kvh * dh)
    return pl.pallas_call(
        _attn_kernel,
        grid=(bsz, kvh, s // tq),
        in_specs=[pl.BlockSpec((None, tq, gw), lambda b, g, i: (b, i, g)),
                  pl.BlockSpec((None, nk, tk, dh), lambda b, g, i: (b, 0, 0, g)),
                  pl.BlockSpec((None, nk, vrows, tk), lambda b, g, i: (b, 0, g, 0))],
        out_specs=pl.BlockSpec((None, tq, gw), lambda b, g, i: (b, i, g)),
        out_shape=jax.ShapeDtypeStruct((bsz, s, q_inner), BF16),
        scratch_shapes=[pltpu.VMEM((dh, group * tq), BF16),
                        pltpu.VMEM((1, group * tq), F32),
                        pltpu.VMEM((2, 1, group * tq), F32),
                        pltpu.VMEM((vrows, group * tq), F32),
                        pltpu.VMEM((2, tk, group * tq), BF16),
                        pltpu.VMEM((2, tk, group * tq), BF16)],
        compiler_params=_params("parallel", "parallel", "parallel"),
        name="attn",
    )(q, k4, vt)


def _post_kernel(hf_ref, hb_ref, o_ref, bm_ref, ba_ref, ya_ref, x_ref, mod_ref, gn_ref, n1_ref, wm_ref,
                 wa_ref, wo_ref, n2a_ref, n2b_ref, w1_ref, w2_ref, out_ref, ym_scr, x1_scr, *, ff_chunk):
    d = x_ref.shape[-1]
    dh = d // MLSTM_HEADS
    pa = _dot(ya_ref[...], wa_ref[...])
    pm = None
    for head in range(MLSTM_HEADS):
        sl = slice(head * dh, (head + 1) * dh)
        h = hf_ref[:, sl] + hb_ref[:, sl]
        mu = jnp.mean(h, axis=-1, keepdims=True)
        hc = h - mu
        var = jnp.mean(hc * hc, axis=-1, keepdims=True)
        hn = hc * lax.rsqrt(var + EPS) * gn_ref[:, sl]
        ym_scr[:, sl] = (hn * _sigmoid(o_ref[:, sl].astype(F32))).astype(BF16)
        part = _dot(ym_scr[:, sl], wm_ref[sl, :])
        pm = part if pm is None else pm + part
    y = _sigmoid(bm_ref[...].astype(F32)) * pm + _sigmoid(ba_ref[...].astype(F32)) * pa
    y2 = _dot(y.astype(BF16), wo_ref[...])
    ms = jnp.mean(y2 * y2, axis=-1, keepdims=True)
    x1_scr[...] = x_ref[...] + mod_ref[:, 2 * d:3 * d] * (y2 * lax.rsqrt(ms + EPS) * n1_ref[...])

    x1 = x1_scr[...]
    ms1 = jnp.mean(x1 * x1, axis=-1, keepdims=True)
    shift = mod_ref[:, 3 * d:4 * d]
    scale = mod_ref[:, 4 * d:5 * d]
    gate = mod_ref[:, 5 * d:6 * d]
    h2 = (x1 * lax.rsqrt(ms1 + EPS) * n2a_ref[...] * (1.0 + scale) + shift).astype(BF16)
    acc = None
    for c in range(w1_ref.shape[1] // ff_chunk):
        sl = slice(c * ff_chunk, (c + 1) * ff_chunk)
        u = jnp.maximum(_dot(h2, w1_ref[:, sl]), 0.0)
        part = _dot((u * u).astype(BF16), w2_ref[sl, :])
        acc = part if acc is None else acc + part
    ms2 = jnp.mean(acc * acc, axis=-1, keepdims=True)
    out_ref[...] = x1_scr[...] + gate * (acc * lax.rsqrt(ms2 + EPS) * n2b_ref[...])


def _post(hf, hb, mid, ya, x, mod3, gn_w, n1_post, wm, wa, wo, n2_pre, n2_post, w1, w2,
          o_block, bm_block, ba_block, tm, ff_chunk):
    bsz, s, d = x.shape
    ff = w1.shape[1]
    tok = lambda b, t: (b, t, 0)
    const = lambda b, t: (0, 0)
    resident = lambda shape: pl.BlockSpec(shape, const, pipeline_mode=pl.Buffered(1))
    kern = functools.partial(_post_kernel, ff_chunk=ff_chunk)
    return pl.pallas_call(
        kern,
        grid=(bsz, s // tm),
        in_specs=[pl.BlockSpec((None, tm, d), tok),
                  pl.BlockSpec((None, tm, d), tok),
                  pl.BlockSpec((None, tm, d), lambda b, t: (b, t, o_block)),
                  pl.BlockSpec((None, tm, d), lambda b, t: (b, t, bm_block)),
                  pl.BlockSpec((None, tm, d), lambda b, t: (b, t, ba_block)),
                  pl.BlockSpec((None, tm, d), tok),
                  pl.BlockSpec((None, tm, d), tok),
                  pl.BlockSpec((None, 1, mod3.shape[-1]), lambda b, t: (b, 0, 0)),
                  pl.BlockSpec((1, d), const),
                  pl.BlockSpec((1, d), const),
                  resident((d, d)), resident((d, d)), resident((d, d)),
                  pl.BlockSpec((1, d), const),
                  pl.BlockSpec((1, d), const),
                  resident((d, ff)), resident((ff, d))],
        out_specs=pl.BlockSpec((None, tm, d), tok),
        out_shape=jax.ShapeDtypeStruct((bsz, s, d), F32),
        scratch_shapes=[pltpu.VMEM((tm, d), BF16), pltpu.VMEM((tm, d), F32)],
        compiler_params=_params("parallel", "parallel"),
        name="post",
    )(hf, hb, mid, mid, mid, ya, x, mod3, gn_w, n1_post, wm, wa, wo, n2_pre, n2_post, w1, w2)


def _layer(x, mod, norm1_pre, norm1_post, w_in, b_gates, conv_w, conv_b, mlstm_gn, attn_qnorm,
           attn_knorm, w_branch_m, w_branch_a, w_out, norm2_pre, norm2_post, w_mlp_in, w_mlp_out):
    bsz, s, d = x.shape
    inner = d
    n_gate = 4 * MLSTM_HEADS
    q_inner = d
    kv_inner = ATTN_KV_HEADS * ATTN_HEAD_DIM

    o_g = 4 * inner
    o_qa = o_g + n_gate
    o_ka = o_qa + q_inner
    o_va = o_ka + kv_inner
    o_br = o_va + kv_inner
    perm = _rope_head_perm()
    head_perm = lambda w: w.reshape(d, -1, ATTN_HEAD_DIM)[:, :, perm].reshape(d, -1)
    w_parts = [w_in[:, 0:o_g], w_in[:, o_br:o_br + 2 * d], head_perm(w_in[:, o_qa:o_va]), w_in[:, o_va:o_br]]
    w_main = jnp.concatenate([part.astype(BF16) for part in w_parts], axis=1)
    conv_cols, mid_cols = 2 * inner, inner + 2 * d
    hm = MLSTM_HEADS
    wg = w_in[:, o_g:o_qa]
    gate_cols = lambda a: (jnp.concatenate([a[..., 0:hm], a[..., 2 * hm:3 * hm]], -1),
                           jnp.concatenate([a[..., hm:2 * hm], a[..., 3 * hm:4 * hm]], -1))
    wi, wf = gate_cols(wg)
    bi, bf = gate_cols(b_gates)
    lane_pad = lambda a: jnp.pad(a, [(0, 0)] * (a.ndim - 1) + [(0, LANES - a.shape[-1])])
    w_gates = jnp.concatenate([lane_pad(wi), lane_pad(wf)], -1).astype(BF16)
    bias_gates = jnp.concatenate([lane_pad(bi), lane_pad(bf)], -1).reshape(1, 2 * LANES)

    mod3 = mod.reshape(bsz, 1, mod.shape[-1])
    cos, sin = _rope_tables(s)
    dh_m = inner // MLSTM_HEADS
    qk, vt_m, mid, q_rot, k_rot, v_t, gates = _in_proj(
        x, mod3, norm1_pre.reshape(1, d), w_main, w_gates, bias_gates, conv_w, conv_b, cos, sin,
        attn_qnorm[perm], attn_knorm[perm], conv_cols=conv_cols, mid_cols=mid_cols, qa_cols=q_inner,
        ka_cols=kv_inner, tm=min(s, 512), q_scale_m=dh_m ** -0.5,
        q_scale_a=ATTN_HEAD_DIM ** -0.5 * math.log2(math.e))

    h_f, h_b = _mlstm(qk, vt_m, gates, ln=min(s, 256))
    y_a = _attention(q_rot, k_rot, v_t, tq=min(s, 1024))

    return _post(h_f, h_b, mid, y_a, x, mod3, mlstm_gn.reshape(1, d), norm1_post.reshape(1, d),
                 w_branch_m.astype(BF16), w_branch_a.astype(BF16), w_out.astype(BF16),
                 norm2_pre.reshape(1, d), norm2_post.reshape(1, d), w_mlp_in.astype(BF16),
                 w_mlp_out.astype(BF16), o_block=0, bm_block=1, ba_block=2, tm=min(s, 512), ff_chunk=1024)


def kernel(x, c, w_ada, b_ada, norm1_pre, norm1_post, w_in, b_gates, conv_w, conv_b, mlstm_gn, attn_qnorm, attn_knorm, w_branch_m, w_branch_a, w_out, norm2_pre, norm2_post, w_mlp_in, w_mlp_out):
    for l in range(w_ada.shape[0]):
        mod = _ada(c, w_ada[l], b_ada[l])
        x = _layer(x, mod, norm1_pre[l], norm1_post[l], w_in[l], b_gates[l], conv_w[l], conv_b[l],
                   mlstm_gn[l], attn_qnorm[l], attn_knorm[l], w_branch_m[l], w_branch_a[l], w_out[l],
                   norm2_pre[l], norm2_post[l], w_mlp_in[l], w_mlp_out[l])
    return x
```
